```python
import jax, jax.numpy as jnp
from jax import lax
import numpy as np

D_MODEL = 1024
BATCH = 16
SEQ = 2048
DEPTH = 1
DEC_BATCH = 8
DEC_SEQ = 2048
PAST_LEN = 128

D_CONV = 512
CONV_W = 3
N_HEADS = 16
N_KV_HEADS = 4
GROUP = N_HEADS // N_KV_HEADS
HEAD_DIM = 64
AXIS_DIM = HEAD_DIM // 2
ROPE_THETA = 10000.0
GRID_W = 64
Q_BLOCK = 128
N_MEM = 256
N_MEM_HEADS = 4
MEM_HEAD_DIM = 128
N_BRANCH = 3
D_FF = 2816
EPS = 1e-6
SPLIT_WIDTHS = (D_CONV, D_CONV, D_CONV, N_HEADS * HEAD_DIM, N_KV_HEADS * HEAD_DIM,
                N_KV_HEADS * HEAD_DIM, N_MEM_HEADS * MEM_HEAD_DIM, N_BRANCH * D_MODEL)
D_IN_PROJ = 3 * D_CONV + (N_HEADS + 2 * N_KV_HEADS) * HEAD_DIM + N_MEM_HEADS * MEM_HEAD_DIM + N_BRANCH * D_MODEL

kernel_name = "hybrid_gated_conv_axialgqa_memxattn_encoder"


def rmsnorm(x, g):
    xf = x.astype(jnp.float32)
    y = xf * lax.rsqrt(jnp.mean(xf * xf, axis=-1, keepdims=True) + EPS)
    return (y * g.astype(jnp.float32)).astype(x.dtype)


def swiglu(x, w1, w3, w2):
    return (jax.nn.silu(x @ w1) * (x @ w3)) @ w2


def axial_rope_tables(T):
    rows = T // GRID_W
    row = jnp.repeat(jnp.arange(rows), GRID_W).astype(jnp.float32)
    col = jnp.tile(jnp.arange(GRID_W), rows).astype(jnp.float32)
    inv = 1.0 / (ROPE_THETA ** (jnp.arange(0, AXIS_DIM, 2, dtype=jnp.float32) / AXIS_DIM))
    ang = jnp.concatenate([row[:, None] * inv, col[:, None] * inv], axis=-1)
    return jnp.cos(ang)[:, None, :], jnp.sin(ang)[:, None, :]


def apply_rope(x, cos, sin):
    xf = x.astype(jnp.float32)
    x1, x2 = xf[..., 0::2], xf[..., 1::2]
    out = jnp.stack([x1 * cos - x2 * sin, x1 * sin + x2 * cos], axis=-1)
    return out.reshape(x.shape).astype(x.dtype)


def self_attention(q, k, v):
    B, T = q.shape[0], q.shape[1]
    nblk = T // Q_BLOCK
    qb = q.reshape(B, nblk, Q_BLOCK, N_KV_HEADS, GROUP, HEAD_DIM).transpose(1, 0, 2, 3, 4, 5)
    scale = HEAD_DIM ** -0.5

    def one_block(qblk):
        s = jnp.einsum('bqkgd,bskd->bkgqs', qblk, k).astype(jnp.float32) * scale
        p = jax.nn.softmax(s, axis=-1).astype(v.dtype)
        return jnp.einsum('bkgqs,bskd->bqkgd', p, v)

    o = lax.map(one_block, qb)
    return o.transpose(1, 0, 2, 3, 4, 5).reshape(B, T, N_HEADS * HEAD_DIM)


def memory_attention(qm, mem, mem_norm, w_mem_kv):
    B, T = qm.shape[0], qm.shape[1]
    M = mem.shape[1]
    kv = rmsnorm(mem, mem_norm) @ w_mem_kv
    km, vm = jnp.split(kv, 2, axis=-1)
    km = km.reshape(B, M, N_MEM_HEADS, MEM_HEAD_DIM)
    vm = vm.reshape(B, M, N_MEM_HEADS, MEM_HEAD_DIM)
    qh = qm.reshape(B, T, N_MEM_HEADS, MEM_HEAD_DIM)
    s = jnp.einsum('bqhd,bmhd->bhqm', qh, km).astype(jnp.float32) * (MEM_HEAD_DIM ** -0.5)
    p = jax.nn.softmax(s, axis=-1).astype(vm.dtype)
    return jnp.einsum('bhqm,bmhd->bqhd', p, vm).reshape(B, T, N_MEM_HEADS * MEM_HEAD_DIM)


def encoder_layer(x, mem, p):
    B, T, _ = x.shape
    h = rmsnorm(x, p['ffn1_pre'])
    x = x + 0.5 * rmsnorm(swiglu(h, p['ffn1_w1'], p['ffn1_w3'], p['ffn1_w2']), p['ffn1_post'])

    u = rmsnorm(x, p['mix_pre'])
    proj = u @ p['w_in']
    offs, acc = [], 0
    for w in SPLIT_WIDTHS[:-1]:
        acc += w
        offs.append(acc)
    cx, cb, cc, q, k, v, qm, g = jnp.split(proj, offs, axis=-1)

    z = cc * cx
    zp = jnp.pad(z, ((0, 0), (1, 1), (0, 0)))
    cw = p['conv_w']
    zc = zp[:, :-2] * cw[0] + zp[:, 1:-1] * cw[1] + zp[:, 2:] * cw[2] + p['conv_b']
    y_conv = (cb * zc) @ p['p_conv']

    q = rmsnorm(q.reshape(B, T, N_HEADS, HEAD_DIM), p['q_norm'])
    k = rmsnorm(k.reshape(B, T, N_KV_HEADS, HEAD_DIM), p['k_norm'])
    v = v.reshape(B, T, N_KV_HEADS, HEAD_DIM)
    cos, sin = axial_rope_tables(T)
    q = apply_rope(q, cos, sin)
    k = apply_rope(k, cos, sin)
    y_attn = self_attention(q, k, v) @ p['p_attn']

    y_mem = memory_attention(qm, mem, p['mem_norm'], p['w_mem_kv']) @ p['p_mem']

    gates = jax.nn.sigmoid(g + p['b_gate']).reshape(B, T, N_BRANCH, D_MODEL)
    merged = gates[:, :, 0] * y_conv + gates[:, :, 1] * y_attn + gates[:, :, 2] * y_mem
    x = x + rmsnorm(merged @ p['w_out'], p['mix_post'])

    h = rmsnorm(x, p['ffn2_pre'])
    x = x + 0.5 * rmsnorm(swiglu(h, p['ffn2_w1'], p['ffn2_w3'], p['ffn2_w2']), p['ffn2_post'])
    return x


def setup_inputs(seed: int = 0) -> dict:
    key = jax.random.key(seed)
    ks = iter(jax.random.split(key, 40))

    def nrm(shape, scale):
        return jax.random.normal(next(ks), shape, jnp.float32) * scale

    def gain(n):
        return 1.0 + nrm((DEPTH, n), 0.02)

    L, D = DEPTH, D_MODEL
    return {
        'x_prompt': nrm((BATCH, SEQ, D), 1.0),
        'x_sample': nrm((DEC_BATCH, DEC_SEQ, D), 1.0),
        'mem_prompt': nrm((BATCH, N_MEM, D), 1.0),
        'mem_sample': nrm((DEC_BATCH, N_MEM, D), 1.0),
        'ffn1_pre': gain(D),
        'ffn1_w1': nrm((L, D, D_FF), D ** -0.5),
        'ffn1_w3': nrm((L, D, D_FF), D ** -0.5),
        'ffn1_w2': nrm((L, D_FF, D), D_FF ** -0.5),
        'ffn1_post': gain(D),
        'mix_pre': gain(D),
        'w_in': nrm((L, D, D_IN_PROJ), D ** -0.5),
        'conv_w': nrm((L, CONV_W, D_CONV), CONV_W ** -0.5),
        'conv_b': nrm((L, D_CONV), 0.02),
        'p_conv': nrm((L, D_CONV, D), D_CONV ** -0.5),
        'q_norm': gain(HEAD_DIM),
        'k_norm': gain(HEAD_DIM),
        'p_attn': nrm((L, N_HEADS * HEAD_DIM, D), (N_HEADS * HEAD_DIM) ** -0.5),
        'mem_norm': gain(D),
        'w_mem_kv': nrm((L, D, 2 * N_MEM_HEADS * MEM_HEAD_DIM), D ** -0.5),
        'p_mem': nrm((L, N_MEM_HEADS * MEM_HEAD_DIM, D), (N_MEM_HEADS * MEM_HEAD_DIM) ** -0.5),
        'b_gate': nrm((L, N_BRANCH * D), 0.02),
        'w_out': nrm((L, D, D), D ** -0.5),
        'mix_post': gain(D),
        'ffn2_pre': gain(D),
        'ffn2_w1': nrm((L, D, D_FF), D ** -0.5),
        'ffn2_w3': nrm((L, D, D_FF), D ** -0.5),
        'ffn2_w2': nrm((L, D_FF, D), D_FF ** -0.5),
        'ffn2_post': gain(D),
    }


def reference(x_prompt, x_sample, mem_prompt, mem_sample,
              ffn1_pre, ffn1_w1, ffn1_w3, ffn1_w2, ffn1_post,
              mix_pre, w_in, conv_w, conv_b, p_conv, q_norm, k_norm, p_attn,
              mem_norm, w_mem_kv, p_mem, b_gate, w_out, mix_post,
              ffn2_pre, ffn2_w1, ffn2_w3, ffn2_w2, ffn2_post):
    params = {
        'ffn1_pre': ffn1_pre, 'ffn1_w1': ffn1_w1, 'ffn1_w3': ffn1_w3, 'ffn1_w2': ffn1_w2,
        'ffn1_post': ffn1_post, 'mix_pre': mix_pre, 'w_in': w_in, 'conv_w': conv_w,
        'conv_b': conv_b, 'p_conv': p_conv, 'q_norm': q_norm, 'k_norm': k_norm,
        'p_attn': p_attn, 'mem_norm': mem_norm, 'w_mem_kv': w_mem_kv, 'p_mem': p_mem,
        'b_gate': b_gate, 'w_out': w_out, 'mix_post': mix_post, 'ffn2_pre': ffn2_pre,
        'ffn2_w1': ffn2_w1, 'ffn2_w3': ffn2_w3, 'ffn2_w2': ffn2_w2, 'ffn2_post': ffn2_post,
    }
    y_prompt = x_prompt
    y_sample = x_sample
    for l in range(DEPTH):
        p = {name: arr[l] for name, arr in params.items()}
        y_prompt = encoder_layer(y_prompt, mem_prompt, p)
        y_sample = encoder_layer(y_sample, mem_sample, p)
    return (y_prompt, y_sample)
```

```python
import functools

import jax
import jax.numpy as jnp
from jax import lax
from jax.experimental import pallas as pl
from jax.experimental.pallas import tpu as pltpu

D_MODEL = 1024
D_CONV = 512
N_HEADS = 16
N_KV_HEADS = 4
GROUP = N_HEADS // N_KV_HEADS
HEAD_DIM = 64
AXIS_DIM = HEAD_DIM // 2
ROPE_THETA = 10000.0
GRID_W = 64
N_MEM = 256
N_MEM_HEADS = 4
MEM_HEAD_DIM = 128
N_BRANCH = 3
D_FF = 2816
EPS = 1e-6

LANES = 128
MXU_N = 256
GROUP_W = GROUP * HEAD_DIM
D_MEM = N_MEM_HEADS * MEM_HEAD_DIM

FFN_TM = 512
FFN_CK = MXU_N
ROW_CHUNK = 512
ATTN_TQ = 128

F32 = jnp.float32
BF16 = jnp.bfloat16

_VMEM_LIMIT = 56 * 1024 * 1024


def _resident(shape):
    zeros = (0,) * len(shape)
    return pl.BlockSpec(shape, lambda *_: zeros, pipeline_mode=pl.Buffered(1))


def _rms(x, g):
    ms = jnp.mean(x * x, axis=-1, keepdims=True)
    return x * lax.rsqrt(ms + EPS) * g


def _ffn_kernel(x_ref, pre_ref, w13_ref, w2_ref, post_ref, mixg_ref, o_ref, *rest, emit_u):
    if emit_u:
        u_ref, h_scr = rest
    else:
        (h_scr,) = rest
    x = x_ref[...]
    xn = _rms(x, pre_ref[...]).astype(BF16)
    for c in range(D_FF // FFN_CK):
        r = jnp.dot(xn, w13_ref[c], preferred_element_type=F32)
        a = r[:, :FFN_CK]
        b = r[:, FFN_CK:]
        h = a * (1.0 / (1.0 + jnp.exp(-a))) * b
        h_scr[:, c * FFN_CK:(c + 1) * FFN_CK] = h.astype(BF16)
    y = jnp.dot(h_scr[...], w2_ref[...], preferred_element_type=F32)
    xo = x + 0.5 * _rms(y, post_ref[...])
    o_ref[...] = xo
    if emit_u:
        u_ref[...] = _rms(xo, mixg_ref[...]).astype(BF16)


def _ffn(x2d, pre, w13, w2, post, mixg, emit_u):
    n = x2d.shape[0]
    tm = FFN_TM
    row = lambda i: (i, 0)
    out_shape = [jax.ShapeDtypeStruct((n, D_MODEL), F32)]
    out_specs = [pl.BlockSpec((tm, D_MODEL), row)]
    if emit_u:
        out_shape.append(jax.ShapeDtypeStruct((n, D_MODEL), BF16))
        out_specs.append(pl.BlockSpec((tm, D_MODEL), row))
    res = pl.pallas_call(
        functools.partial(_ffn_kernel, emit_u=emit_u),
        grid=(n // tm,),
        in_specs=[
            pl.BlockSpec((tm, D_MODEL), row),
            _resident((1, D_MODEL)),
            _resident(w13.shape),
            _resident(w2.shape),
            _resident((1, D_MODEL)),
            _resident((1, D_MODEL)),
        ],
        out_specs=out_specs,
        out_shape=out_shape,
        scratch_shapes=[pltpu.VMEM((tm, D_FF), BF16)],
        compiler_params=pltpu.CompilerParams(
            dimension_semantics=("arbitrary",), vmem_limit_bytes=_VMEM_LIMIT),
        name="ffn_u" if emit_u else "ffn",
    )(x2d, pre, w13, w2, post, mixg)
    return res


def _pair_swap(x):
    lane = lax.broadcasted_iota(jnp.int32, x.shape, 1)
    nxt = pltpu.roll(x, LANES - 1, 1)
    prv = pltpu.roll(x, 1, 1)
    return jnp.where((lane & 1) == 0, nxt, prv)


def _norm_rope(r, ssum_ref, g_ref, cos, sin, scale):
    ss = jnp.dot((r * r).astype(BF16), ssum_ref[...], preferred_element_type=F32)
    rinv = lax.rsqrt(ss * (1.0 / HEAD_DIM) + EPS) * scale
    xg = r * g_ref[...]
    outs = []
    for j in range(r.shape[1] // LANES):
        xs = xg[:, j * LANES:(j + 1) * LANES]
        outs.append(xs * cos + _pair_swap(xs) * sin)
    return jnp.concatenate(outs, axis=1) * rinv


def _group_tile(x, g):
    pair = x[:, (g // 2) * LANES:(g // 2 + 1) * LANES]
    rot = pltpu.roll(pair, HEAD_DIM, 1)
    lane = lax.broadcasted_iota(jnp.int32, pair.shape, 1)
    lo = lane < HEAD_DIM
    t = jnp.where(lo, pair, rot) if g % 2 == 0 else jnp.where(lo, rot, pair)
    return jnp.concatenate([t, t], axis=1)


def _inproj_kernel(u_ref, mem_ref, wc_ref, wq_ref, wkv_ref, wqm_ref, wmkv_ref,
                   convw_ref, convb_ref, qg_ref, kg_ref, memg_ref, cos_ref, sin_ref, sq_ref, sk_ref,
                   cbz_ref, q_ref, kt_ref, vt_ref, qm_ref, km_ref, vm_ref,
                   z_scr, cb_scr):
    t_len = cbz_ref.shape[0]
    n_chunks = t_len // ROW_CHUNK
    j = pl.program_id(1)

    @pl.when(j == 0)
    def _():
        mn = _rms(mem_ref[...], memg_ref[...]).astype(BF16)
        kv = jnp.dot(mn, wmkv_ref[...], preferred_element_type=F32)
        km_ref[...] = kv[:, :D_MEM].astype(BF16)
        vm_ref[...] = kv[:, D_MEM:].astype(BF16)
        zero_rows = jnp.zeros((8, D_CONV), F32)
        z_scr[0:8, :] = zero_rows
        z_scr[t_len + 8:t_len + 16, :] = zero_rows

    r0 = pl.multiple_of(j * ROW_CHUNK, ROW_CHUNK)
    uc = u_ref[...]

    rc = jnp.dot(uc, wc_ref[...], preferred_element_type=F32)
    z_scr[pl.ds(r0 + 8, ROW_CHUNK), :] = rc[:, 2 * D_CONV:] * rc[:, :D_CONV]
    cb_scr[pl.ds(r0, ROW_CHUNK), :] = rc[:, D_CONV:2 * D_CONV].astype(BF16)

    cos = cos_ref[pl.ds(r0, ROW_CHUNK), :]
    sin = sin_ref[pl.ds(r0, ROW_CHUNK), :]
    rq = jnp.dot(uc, wq_ref[...], preferred_element_type=F32)
    q_ref[...] = _norm_rope(rq, sq_ref, qg_ref, cos, sin, HEAD_DIM ** -0.5).astype(BF16)

    rkv = jnp.dot(uc, wkv_ref[...], preferred_element_type=F32)
    kk = _norm_rope(rkv[:, :GROUP_W], sk_ref, kg_ref, cos, sin, 1.0)
    vv = rkv[:, GROUP_W:]
    for g in range(N_KV_HEADS):
        kt_ref[g] = _group_tile(kk, g).astype(BF16)
        vt_ref[g] = _group_tile(vv, g).astype(BF16)

    rqm = jnp.dot(uc, wqm_ref[...], preferred_element_type=F32)
    qm_ref[...] = (rqm * (MEM_HEAD_DIM ** -0.5)).astype(BF16)

    @pl.when(j == n_chunks - 1)
    def _():
        cw = convw_ref[...]
        for c in range(n_chunks):
            c0 = c * ROW_CHUNK
            zc = (z_scr[c0 + 7:c0 + 7 + ROW_CHUNK, :] * cw[0:1, :]
                  + z_scr[c0 + 8:c0 + 8 + ROW_CHUNK, :] * cw[1:2, :]
                  + z_scr[c0 + 9:c0 + 9 + ROW_CHUNK, :] * cw[2:3, :]
                  + convb_ref[...])
            cbz_ref[c0:c0 + ROW_CHUNK, :] = (cb_scr[c0:c0 + ROW_CHUNK, :].astype(F32) * zc).astype(BF16)


def _inproj(u, mem, w, tables):
    b, t, _ = u.shape
    cos, sin, sq, sk = tables
    rows = lambda width: pl.BlockSpec((None, ROW_CHUNK, width), lambda i, j: (i, j, 0))
    tiled = pl.BlockSpec((None, N_KV_HEADS, ROW_CHUNK, GROUP_W), lambda i, j: (i, 0, j, 0))
    memspec = pl.BlockSpec((None, N_MEM, D_MEM), lambda i, j: (i, 0, 0))
    return pl.pallas_call(
        _inproj_kernel,
        grid=(b, t // ROW_CHUNK),
        in_specs=[
            rows(D_MODEL),
            pl.BlockSpec((None, N_MEM, D_MODEL), lambda i, j: (i, 0, 0)),
            _resident(w["w_c"].shape), _resident(w["w_q"].shape), _resident(w["w_kv"].shape),
            _resident(w["w_qm"].shape), _resident(w["w_mkv"].shape),
            _resident((3, D_CONV)), _resident((1, D_CONV)),
            _resident((1, N_HEADS * HEAD_DIM)), _resident((1, GROUP_W)), _resident((1, D_MODEL)),
            _resident(cos.shape), _resident(sin.shape), _resident(sq.shape), _resident(sk.shape),
        ],
        out_specs=[pl.BlockSpec((None, t, D_CONV), lambda i, j: (i, 0, 0)), rows(N_HEADS * HEAD_DIM),
                   tiled, tiled, rows(D_MEM), memspec, memspec],
        out_shape=[
            jax.ShapeDtypeStruct((b, t, D_CONV), BF16),
            jax.ShapeDtypeStruct((b, t, N_HEADS * HEAD_DIM), BF16),
            jax.ShapeDtypeStruct((b, N_KV_HEADS, t, GROUP_W), BF16),
            jax.ShapeDtypeStruct((b, N_KV_HEADS, t, GROUP_W), BF16),
            jax.ShapeDtypeStruct((b, t, D_MEM), BF16),
            jax.ShapeDtypeStruct((b, N_MEM, D_MEM), BF16),
            jax.ShapeDtypeStruct((b, N_MEM, D_MEM), BF16),
        ],
        scratch_shapes=[pltpu.VMEM((t + 16, D_CONV), F32), pltpu.VMEM((t, D_CONV), BF16)],
        compiler_params=pltpu.CompilerParams(
            dimension_semantics=("arbitrary", "arbitrary"), vmem_limit_bytes=_VMEM_LIMIT),
        name="inproj",
    )(u, mem, w["w_c"], w["w_q"], w["w_kv"], w["w_qm"], w["w_mkv"],
      w["conv_w"], w["conv_b"], w["q_gain"], w["k_gain"], w["mem_norm"], cos, sin, sq, sk)


def _attn_kernel(q_ref, kt_ref, vt_ref, o_ref):
    tq = q_ref.shape[0]
    q = q_ref[...].astype(F32)
    head = lax.broadcasted_iota(jnp.int32, q.shape, 1) // HEAD_DIM
    qs = jnp.concatenate([jnp.where(head == h, q, 0.0) for h in range(GROUP)], axis=0).astype(BF16)
    s = lax.dot_general(qs, kt_ref[...], (((1,), (1,)), ((), ())), preferred_element_type=F32)
    m = jnp.max(s, axis=-1, keepdims=True)
    p = jnp.exp(s - m)
    l = jnp.sum(p, axis=-1, keepdims=True)
    o = jnp.dot(p.astype(BF16), vt_ref[...], preferred_element_type=F32)
    o = o * (1.0 / l)
    out = o[0:tq]
    for h in range(1, GROUP):
        out = jnp.where(head == h, o[h * tq:(h + 1) * tq], out)
    o_ref[...] = out.astype(BF16)


def _attn(q, kt, vt):
    b, t, _ = q.shape
    tq = ATTN_TQ
    kv_spec = pl.BlockSpec((None, None, t, GROUP_W), lambda i, g, j: (i, g, 0, 0))
    return pl.pallas_call(
        _attn_kernel,
        grid=(b, N_KV_HEADS, t // tq),
        in_specs=[pl.BlockSpec((None, tq, GROUP_W), lambda i, g, j: (i, j, g)), kv_spec, kv_spec],
        out_specs=pl.BlockSpec((None, tq, GROUP_W), lambda i, g, j: (i, j, g)),
        out_shape=jax.ShapeDtypeStruct((b, t, N_HEADS * HEAD_DIM), BF16),
        compiler_params=pltpu.CompilerParams(
            dimension_semantics=("arbitrary", "arbitrary", "arbitrary"), vmem_limit_bytes=_VMEM_LIMIT),
        name="attn",
    )(q, kt, vt)


def _merge_kernel(x_ref, cbz_ref, attn_ref, qm_ref, km_ref, vm_ref,
                  pconv_ref, pattn_ref, pmem_ref, wg_ref, bg_ref, wout_ref, mixg_ref, postg_ref, o_ref):
    x = x_ref[...]
    u = _rms(x, mixg_ref[...]).astype(BF16)

    qm = qm_ref[...]
    heads = []
    for h in range(N_MEM_HEADS):
        cols = slice(h * MEM_HEAD_DIM, (h + 1) * MEM_HEAD_DIM)
        s = lax.dot_general(qm[:, cols], km_ref[:, cols], (((1,), (1,)), ((), ())),
                            preferred_element_type=F32)
        p = jnp.exp(s - jnp.max(s, axis=-1, keepdims=True))
        l = jnp.sum(p, axis=-1, keepdims=True)
        oh = jnp.dot(p.astype(BF16), vm_ref[:, cols], preferred_element_type=F32)
        heads.append(oh * (1.0 / l))
    memo = jnp.concatenate(heads, axis=1).astype(BF16)

    branches = (
        (cbz_ref[...], pconv_ref),
        (attn_ref[...], pattn_ref),
        (memo, pmem_ref),
    )
    merged = None
    for br, (act, proj_ref) in enumerate(branches):
        cols = slice(br * D_MODEL, (br + 1) * D_MODEL)
        gate = jnp.dot(u, wg_ref[:, cols], preferred_element_type=F32) + bg_ref[:, cols]
        gate = 1.0 / (1.0 + jnp.exp(-gate))
        term = gate * jnp.dot(act, proj_ref[...], preferred_element_type=F32)
        merged = term if merged is None else merged + term
    y = jnp.dot(merged.astype(BF16), wout_ref[...], preferred_element_type=F32)
    o_ref[...] = x + _rms(y, postg_ref[...])


def _merge(x, cbz, attn, qm, km, vm, w):
    b, t, _ = x.shape
    tm = FFN_TM
    tok = lambda width: pl.BlockSpec((None, tm, width), lambda i, j: (i, j, 0))
    memspec = pl.BlockSpec((None, N_MEM, D_MEM), lambda i, j: (i, 0, 0))
    return pl.pallas_call(
        _merge_kernel,
        grid=(b, t // tm),
        in_specs=[
            tok(D_MODEL), tok(D_CONV), tok(N_HEADS * HEAD_DIM), tok(D_MEM), memspec, memspec,
            _resident(w["p_conv"].shape), _resident(w["p_attn"].shape), _resident(w["p_mem"].shape),
            _resident(w["w_g"].shape), _resident((1, N_BRANCH * D_MODEL)), _resident(w["w_out"].shape),
            _resident((1, D_MODEL)), _resident((1, D_MODEL)),
        ],
        out_specs=tok(D_MODEL),
        out_shape=jax.ShapeDtypeStruct((b, t, D_MODEL), F32),
        compiler_params=pltpu.CompilerParams(
            dimension_semantics=("arbitrary", "arbitrary"), vmem_limit_bytes=_VMEM_LIMIT),
        name="merge",
    )(x, cbz, attn, qm, km, vm, w["p_conv"], w["p_attn"], w["p_mem"], w["w_g"], w["b_gate"],
      w["w_out"], w["mix_pre"], w["mix_post"])


def _rope_tables(t):
    pos = jnp.arange(t)
    row = (pos // GRID_W).astype(F32)
    col = (pos % GRID_W).astype(F32)
    inv = 1.0 / (ROPE_THETA ** (jnp.arange(0, AXIS_DIM, 2, dtype=F32) / AXIS_DIM))
    ang = jnp.concatenate([row[:, None] * inv, col[:, None] * inv], axis=-1)
    cos = jnp.repeat(jnp.cos(ang), 2, axis=-1)
    sin = jnp.stack([-jnp.sin(ang), jnp.sin(ang)], axis=-1).reshape(t, HEAD_DIM)
    reps = LANES // HEAD_DIM
    return jnp.tile(cos, (1, reps)), jnp.tile(sin, (1, reps))


def _head_sum_matrix(width):
    head = jnp.arange(width) // HEAD_DIM
    return (head[:, None] == head[None, :]).astype(BF16)


def _swiglu_weights(w1, w3):
    n = D_FF // FFN_CK
    a = w1.astype(BF16).reshape(D_MODEL, n, FFN_CK)
    b = w3.astype(BF16).reshape(D_MODEL, n, FFN_CK)
    return jnp.concatenate([a, b], axis=-1).transpose(1, 0, 2)


def _layer(x, mem, w, tables):
    b, t, d = x.shape
    x1, u = _ffn(x.reshape(b * t, d), w["ffn1_pre"], w["ffn1_w13"], w["ffn1_w2"], w["ffn1_post"],
                 w["mix_pre"], emit_u=True)
    cbz, q, kt, vt, qm, km, vm = _inproj(u.reshape(b, t, d), mem, w, tables)
    attn = _attn(q, kt, vt)
    x2 = _merge(x1.reshape(b, t, d), cbz, attn, qm, km, vm, w)
    (y,) = _ffn(x2.reshape(b * t, d), w["ffn2_pre"], w["ffn2_w13"], w["ffn2_w2"], w["ffn2_post"],
                w["mix_pre"], emit_u=False)
    return y.reshape(b, t, d)


def kernel(x_prompt, x_sample, mem_prompt, mem_sample, ffn1_pre, ffn1_w1, ffn1_w3, ffn1_w2, ffn1_post, mix_pre, w_in, conv_w, conv_b, p_conv, q_norm, k_norm, p_attn, mem_norm, w_mem_kv, p_mem, b_gate, w_out, mix_post, ffn2_pre, ffn2_w1, ffn2_w3, ffn2_w2, ffn2_post):
    y_prompt, y_sample = x_prompt, x_sample
    for l in range(ffn1_pre.shape[0]):
        win = w_in[l].astype(BF16)
        o_q = 3 * D_CONV
        o_k = o_q + N_HEADS * HEAD_DIM
        o_qm = o_k + 2 * GROUP_W
        o_g = o_qm + D_MEM
        w = {
            "ffn1_pre": ffn1_pre[l][None], "ffn1_post": ffn1_post[l][None],
            "ffn1_w13": _swiglu_weights(ffn1_w1[l], ffn1_w3[l]), "ffn1_w2": ffn1_w2[l].astype(BF16),
            "ffn2_pre": ffn2_pre[l][None], "ffn2_post": ffn2_post[l][None],
            "ffn2_w13": _swiglu_weights(ffn2_w1[l], ffn2_w3[l]), "ffn2_w2": ffn2_w2[l].astype(BF16),
            "mix_pre": mix_pre[l][None], "mix_post": mix_post[l][None],
            "w_c": win[:, :o_q], "w_q": win[:, o_q:o_k], "w_kv": win[:, o_k:o_qm], "w_qm": win[:, o_qm:o_g],
            "w_g": win[:, o_g:], "b_gate": b_gate[l][None],
            "conv_w": conv_w[l], "conv_b": conv_b[l][None],
            "q_gain": jnp.tile(q_norm[l], N_HEADS)[None], "k_gain": jnp.tile(k_norm[l], N_KV_HEADS)[None],
            "mem_norm": mem_norm[l][None], "w_mkv": w_mem_kv[l].astype(BF16),
            "p_conv": p_conv[l].astype(BF16), "p_attn": p_attn[l].astype(BF16), "p_mem": p_mem[l].astype(BF16),
            "w_out": w_out[l].astype(BF16),
        }
        tables = _rope_tables(y_prompt.shape[1]) + (_head_sum_matrix(N_HEADS * HEAD_DIM), _head_sum_matrix(GROUP_W))
        y_prompt = _layer(y_prompt, mem_prompt, w, tables)
        tables_s = _rope_tables(y_sample.shape[1]) + tables[2:]
        y_sample = _layer(y_sample, mem_sample, w, tables_s)
    return (y_prompt, y_sample)
```

```python
import functools

import jax
import jax.numpy as jnp
from jax import lax
from jax.experimental import pallas as pl
from jax.experimental.pallas import tpu as pltpu

D_MODEL = 1024
D_CONV = 512
N_HEADS = 16
N_KV_HEADS = 4
GROUP = N_HEADS // N_KV_HEADS
HEAD_DIM = 64
AXIS_DIM = HEAD_DIM // 2
ROPE_THETA = 10000.0
GRID_W = 64
N_MEM = 256
N_MEM_HEADS = 4
MEM_HEAD_DIM = 128
N_BRANCH = 3
D_FF = 2816
EPS = 1e-6

LANES = 128
MXU_N = 256
GROUP_W = GROUP * HEAD_DIM
D_MEM = N_MEM_HEADS * MEM_HEAD_DIM

FFN_TM = 512
FFN_CK = MXU_N
ROW_CHUNK = 512
ATTN_TQ = 128
ATTN_PARTS = 8

F32 = jnp.float32
BF16 = jnp.bfloat16

_VMEM_LIMIT = 56 * 1024 * 1024


def _resident(shape):
    zeros = (0,) * len(shape)
    return pl.BlockSpec(shape, lambda *_: zeros, pipeline_mode=pl.Buffered(1))


def _rms(x, g):
    ms = jnp.mean(x * x, axis=-1, keepdims=True)
    return x * lax.rsqrt(ms + EPS) * g


def _ffn_kernel(x_ref, pre_ref, w13_ref, w2_ref, post_ref, mixg_ref, o_ref, *rest, emit_u):
    if emit_u:
        u_ref, h_scr = rest
    else:
        (h_scr,) = rest
    x = x_ref[...]
    xn = _rms(x, pre_ref[...]).astype(BF16)
    for c in range(D_FF // FFN_CK):
        r = jnp.dot(xn, w13_ref[c], preferred_element_type=F32)
        a = r[:, :FFN_CK]
        b = r[:, FFN_CK:]
        h = a * (1.0 / (1.0 + jnp.exp(-a))) * b
        h_scr[:, c * FFN_CK:(c + 1) * FFN_CK] = h.astype(BF16)
    y = jnp.dot(h_scr[...], w2_ref[...], preferred_element_type=F32)
    xo = x + 0.5 * _rms(y, post_ref[...])
    o_ref[...] = xo
    if emit_u:
        u_ref[...] = _rms(xo, mixg_ref[...]).astype(BF16)


def _ffn(x2d, pre, w13, w2, post, mixg, emit_u):
    n = x2d.shape[0]
    tm = FFN_TM
    row = lambda i: (i, 0)
    out_shape = [jax.ShapeDtypeStruct((n, D_MODEL), F32)]
    out_specs = [pl.BlockSpec((tm, D_MODEL), row)]
    if emit_u:
        out_shape.append(jax.ShapeDtypeStruct((n, D_MODEL), BF16))
        out_specs.append(pl.BlockSpec((tm, D_MODEL), row))
    res = pl.pallas_call(
        functools.partial(_ffn_kernel, emit_u=emit_u),
        grid=(n // tm,),
        in_specs=[
            pl.BlockSpec((tm, D_MODEL), row),
            _resident((1, D_MODEL)),
            _resident(w13.shape),
            _resident(w2.shape),
            _resident((1, D_MODEL)),
            _resident((1, D_MODEL)),
        ],
        out_specs=out_specs,
        out_shape=out_shape,
        scratch_shapes=[pltpu.VMEM((tm, D_FF), BF16)],
        compiler_params=pltpu.CompilerParams(
            dimension_semantics=("arbitrary",), vmem_limit_bytes=_VMEM_LIMIT),
        name="ffn_u" if emit_u else "ffn",
    )(x2d, pre, w13, w2, post, mixg)
    return res


def _pair_swap(x):
    lane = lax.broadcasted_iota(jnp.int32, x.shape, 1)
    nxt = pltpu.roll(x, LANES - 1, 1)
    prv = pltpu.roll(x, 1, 1)
    return jnp.where((lane & 1) == 0, nxt, prv)


def _norm_rope(r, ssum_ref, g_ref, cos, sin, scale):
    ss = jnp.dot((r * r).astype(BF16), ssum_ref[...], preferred_element_type=F32)
    rinv = lax.rsqrt(ss * (1.0 / HEAD_DIM) + EPS) * scale
    xg = r * g_ref[...]
    outs = []
    for j in range(r.shape[1] // LANES):
        xs = xg[:, j * LANES:(j + 1) * LANES]
        outs.append(xs * cos + _pair_swap(xs) * sin)
    return jnp.concatenate(outs, axis=1) * rinv


def _group_tile(x, g):
    pair = x[:, (g // 2) * LANES:(g // 2 + 1) * LANES]
    rot = pltpu.roll(pair, HEAD_DIM, 1)
    lane = lax.broadcasted_iota(jnp.int32, pair.shape, 1)
    lo = lane < HEAD_DIM
    t = jnp.where(lo, pair, rot) if g % 2 == 0 else jnp.where(lo, rot, pair)
    return jnp.concatenate([t, t], axis=1)


def _inproj_kernel(u_ref, mem_ref, wc_ref, wq_ref, wkv_ref, wqm_ref, wmkv_ref,
                   convw_ref, convb_ref, qg_ref, kg_ref, memg_ref, cos_ref, sin_ref, sq_ref, sk_ref,
                   cbz_ref, q_ref, kt_ref, vt_ref, qm_ref, km_ref, vm_ref,
                   z_scr, cb_scr):
    t_len = cbz_ref.shape[0]
    n_chunks = t_len // ROW_CHUNK
    j = pl.program_id(1)

    @pl.when(j == 0)
    def _():
        mn = _rms(mem_ref[...], memg_ref[...]).astype(BF16)
        kv = jnp.dot(mn, wmkv_ref[...], preferred_element_type=F32)
        km_ref[...] = kv[:, :D_MEM].astype(BF16)
        vm_ref[...] = kv[:, D_MEM:].astype(BF16)
        zero_rows = jnp.zeros((8, D_CONV), F32)
        z_scr[0:8, :] = zero_rows
        z_scr[t_len + 8:t_len + 16, :] = zero_rows

    r0 = pl.multiple_of(j * ROW_CHUNK, ROW_CHUNK)
    uc = u_ref[...]

    rc = jnp.dot(uc, wc_ref[...], preferred_element_type=F32)
    z_scr[pl.ds(r0 + 8, ROW_CHUNK), :] = rc[:, 2 * D_CONV:] * rc[:, :D_CONV]
    cb_scr[pl.ds(r0, ROW_CHUNK), :] = rc[:, D_CONV:2 * D_CONV].astype(BF16)

    cos = cos_ref[pl.ds(r0, ROW_CHUNK), :]
    sin = sin_ref[pl.ds(r0, ROW_CHUNK), :]
    rq = jnp.dot(uc, wq_ref[...], preferred_element_type=F32)
    q_ref[...] = _norm_rope(rq, sq_ref, qg_ref, cos, sin, HEAD_DIM ** -0.5).astype(BF16)

    rkv = jnp.dot(uc, wkv_ref[...], preferred_element_type=F32)
    kk = _norm_rope(rkv[:, :GROUP_W], sk_ref, kg_ref, cos, sin, 1.0)
    vv = rkv[:, GROUP_W:]
    for g in range(N_KV_HEADS):
        kt_ref[g] = _group_tile(kk, g).astype(BF16)
        vt_ref[g] = _group_tile(vv, g).astype(BF16)

    rqm = jnp.dot(uc, wqm_ref[...], preferred_element_type=F32)
    qm_ref[...] = (rqm * (MEM_HEAD_DIM ** -0.5)).astype(BF16)

    @pl.when(j == n_chunks - 1)
    def _():
        cw = convw_ref[...]
        for c in range(n_chunks):
            c0 = c * ROW_CHUNK
            zc = (z_scr[c0 + 7:c0 + 7 + ROW_CHUNK, :] * cw[0:1, :]
                  + z_scr[c0 + 8:c0 + 8 + ROW_CHUNK, :] * cw[1:2, :]
                  + z_scr[c0 + 9:c0 + 9 + ROW_CHUNK, :] * cw[2:3, :]
                  + convb_ref[...])
            cbz_ref[c0:c0 + ROW_CHUNK, :] = (cb_scr[c0:c0 + ROW_CHUNK, :].astype(F32) * zc).astype(BF16)


def _inproj(u, mem, w, tables):
    b, t, _ = u.shape
    cos, sin, sq, sk = tables
    rows = lambda width: pl.BlockSpec((None, ROW_CHUNK, width), lambda i, j: (i, j, 0))
    tiled = pl.BlockSpec((None, N_KV_HEADS, ROW_CHUNK, GROUP_W), lambda i, j: (i, 0, j, 0))
    memspec = pl.BlockSpec((None, N_MEM, D_MEM), lambda i, j: (i, 0, 0))
    return pl.pallas_call(
        _inproj_kernel,
        grid=(b, t // ROW_CHUNK),
        in_specs=[
            rows(D_MODEL),
            pl.BlockSpec((None, N_MEM, D_MODEL), lambda i, j: (i, 0, 0)),
            _resident(w["w_c"].shape), _resident(w["w_q"].shape), _resident(w["w_kv"].shape),
            _resident(w["w_qm"].shape), _resident(w["w_mkv"].shape),
            _resident((3, D_CONV)), _resident((1, D_CONV)),
            _resident((1, N_HEADS * HEAD_DIM)), _resident((1, GROUP_W)), _resident((1, D_MODEL)),
            _resident(cos.shape), _resident(sin.shape), _resident(sq.shape), _resident(sk.shape),
        ],
        out_specs=[pl.BlockSpec((None, t, D_CONV), lambda i, j: (i, 0, 0)), rows(N_HEADS * HEAD_DIM),
                   tiled, tiled, rows(D_MEM), memspec, memspec],
        out_shape=[
            jax.ShapeDtypeStruct((b, t, D_CONV), BF16),
            jax.ShapeDtypeStruct((b, t, N_HEADS * HEAD_DIM), BF16),
            jax.ShapeDtypeStruct((b, N_KV_HEADS, t, GROUP_W), BF16),
            jax.ShapeDtypeStruct((b, N_KV_HEADS, t, GROUP_W), BF16),
            jax.ShapeDtypeStruct((b, t, D_MEM), BF16),
            jax.ShapeDtypeStruct((b, N_MEM, D_MEM), BF16),
            jax.ShapeDtypeStruct((b, N_MEM, D_MEM), BF16),
        ],
        scratch_shapes=[pltpu.VMEM((t + 16, D_CONV), F32), pltpu.VMEM((t, D_CONV), BF16)],
        compiler_params=pltpu.CompilerParams(
            dimension_semantics=("arbitrary", "arbitrary"), vmem_limit_bytes=_VMEM_LIMIT),
        name="inproj",
    )(u, mem, w["w_c"], w["w_q"], w["w_kv"], w["w_qm"], w["w_mkv"],
      w["conv_w"], w["conv_b"], w["q_gain"], w["k_gain"], w["mem_norm"], cos, sin, sq, sk)


def _attn_half_step(q, kt_ref, vt_ref, s_new, s_cur, p_new, l_new, p_old, l_old):
    tq = q.shape[0]
    rows, t = s_new.shape
    qf = q.astype(F32)
    head = lax.broadcasted_iota(jnp.int32, qf.shape, 1) // HEAD_DIM
    qs = jnp.concatenate([jnp.where(head == h, qf, 0.0) for h in range(GROUP)], axis=0).astype(BF16)
    kb = t // ATTN_PARTS
    rb = rows // ATTN_PARTS
    acc = None
    for part in range(ATTN_PARTS):
        keys = slice(part * kb, (part + 1) * kb)
        s_new[:, keys] = lax.dot_general(qs, kt_ref[keys, :], (((1,), (1,)), ((), ())),
                                         preferred_element_type=F32)
        rsl = slice(part * rb, (part + 1) * rb)
        s = s_cur[rsl, :]
        p = jnp.exp(s - jnp.max(s, axis=-1, keepdims=True))
        l_new[rsl, :] = jnp.sum(p, axis=-1, keepdims=True)
        p_new[rsl, :] = p.astype(BF16)
        d = jnp.dot(p_old[:, keys], vt_ref[keys, :], preferred_element_type=F32)
        acc = d if acc is None else acc + d
    o = acc * (1.0 / l_old[...])
    out = o[0:tq]
    for h in range(1, GROUP):
        out = jnp.where(head == h, o[h * tq:(h + 1) * tq], out)
    return out.astype(BF16)


def _attn_kernel(q_ref, kt_ref, vt_ref, o_ref, s0, s1, p0, p1, l0, l1):
    n = pl.program_id(0)

    @pl.when(n == 0)
    def _():
        s1[...] = jnp.zeros_like(s1)
        p0[...] = jnp.zeros_like(p0)
        p1[...] = jnp.zeros_like(p1)
        l0[...] = jnp.ones_like(l0)
        l1[...] = jnp.ones_like(l1)

    @pl.when(n % 2 == 0)
    def _():
        o_ref[...] = _attn_half_step(q_ref[...], kt_ref, vt_ref, s0, s1, p1, l1, p0, l0)

    @pl.when(n % 2 == 1)
    def _():
        o_ref[...] = _attn_half_step(q_ref[...], kt_ref, vt_ref, s1, s0, p0, l0, p1, l1)


def _attn(q, kt, vt):
    b, t, _ = q.shape
    tq = ATTN_TQ
    tiles_per_seq = t // tq
    n_tiles = b * N_KV_HEADS * tiles_per_seq

    def place(i):
        return i // (N_KV_HEADS * tiles_per_seq), (i // tiles_per_seq) % N_KV_HEADS, i % tiles_per_seq

    def score_tile(n):
        return place(jnp.minimum(n, n_tiles - 1))

    def value_tile(n):
        return place(jnp.maximum(n - 2, 0))

    def q_map(n):
        bi, g, j = score_tile(n)
        return bi, j, g

    def kt_map(n):
        bi, g, _ = score_tile(n)
        return bi, g, 0, 0

    def vt_map(n):
        bi, g, _ = value_tile(n)
        return bi, g, 0, 0

    def o_map(n):
        bi, g, j = value_tile(n)
        return bi, j, g

    rows = GROUP * tq
    return pl.pallas_call(
        _attn_kernel,
        grid=(n_tiles + 2,),
        in_specs=[pl.BlockSpec((None, tq, GROUP_W), q_map),
                  pl.BlockSpec((None, None, t, GROUP_W), kt_map),
                  pl.BlockSpec((None, None, t, GROUP_W), vt_map)],
        out_specs=pl.BlockSpec((None, tq, GROUP_W), o_map),
        out_shape=jax.ShapeDtypeStruct((b, t, N_HEADS * HEAD_DIM), BF16),
        scratch_shapes=[pltpu.VMEM((rows, t), F32), pltpu.VMEM((rows, t), F32),
                        pltpu.VMEM((rows, t), BF16), pltpu.VMEM((rows, t), BF16),
                        pltpu.VMEM((rows, 1), F32), pltpu.VMEM((rows, 1), F32)],
        compiler_params=pltpu.CompilerParams(
            dimension_semantics=("arbitrary",), vmem_limit_bytes=_VMEM_LIMIT),
        name="attn",
    )(q, kt, vt)


def _merge_kernel(x_ref, cbz_ref, attn_ref, qm_ref, km_ref, vm_ref,
                  pconv_ref, pattn_ref, pmem_ref, wg_ref, bg_ref, wout_ref, mixg_ref, postg_ref, o_ref):
    x = x_ref[...]
    u = _rms(x, mixg_ref[...]).astype(BF16)

    qm = qm_ref[...]
    heads = []
    for h in range(N_MEM_HEADS):
        cols = slice(h * MEM_HEAD_DIM, (h + 1) * MEM_HEAD_DIM)
        s = lax.dot_general(qm[:, cols], km_ref[:, cols], (((1,), (1,)), ((), ())),
                            preferred_element_type=F32)
        p = jnp.exp(s - jnp.max(s, axis=-1, keepdims=True))
        l = jnp.sum(p, axis=-1, keepdims=True)
        oh = jnp.dot(p.astype(BF16), vm_ref[:, cols], preferred_element_type=F32)
        heads.append(oh * (1.0 / l))
    memo = jnp.concatenate(heads, axis=1).astype(BF16)

    branches = (
        (cbz_ref[...], pconv_ref),
        (attn_ref[...], pattn_ref),
        (memo, pmem_ref),
    )
    merged = None
    for br, (act, proj_ref) in enumerate(branches):
        cols = slice(br * D_MODEL, (br + 1) * D_MODEL)
        gate = jnp.dot(u, wg_ref[:, cols], preferred_element_type=F32) + bg_ref[:, cols]
        gate = 1.0 / (1.0 + jnp.exp(-gate))
        term = gate * jnp.dot(act, proj_ref[...], preferred_element_type=F32)
        merged = term if merged is None else merged + term
    y = jnp.dot(merged.astype(BF16), wout_ref[...], preferred_element_type=F32)
    o_ref[...] = x + _rms(y, postg_ref[...])


def _merge(x, cbz, attn, qm, km, vm, w):
    b, t, _ = x.shape
    tm = FFN_TM
    tok = lambda width: pl.BlockSpec((None, tm, width), lambda i, j: (i, j, 0))
    memspec = pl.BlockSpec((None, N_MEM, D_MEM), lambda i, j: (i, 0, 0))
    return pl.pallas_call(
        _merge_kernel,
        grid=(b, t // tm),
        in_specs=[
            tok(D_MODEL), tok(D_CONV), tok(N_HEADS * HEAD_DIM), tok(D_MEM), memspec, memspec,
            _resident(w["p_conv"].shape), _resident(w["p_attn"].shape), _resident(w["p_mem"].shape),
            _resident(w["w_g"].shape), _resident((1, N_BRANCH * D_MODEL)), _resident(w["w_out"].shape),
            _resident((1, D_MODEL)), _resident((1, D_MODEL)),
        ],
        out_specs=tok(D_MODEL),
        out_shape=jax.ShapeDtypeStruct((b, t, D_MODEL), F32),
        compiler_params=pltpu.CompilerParams(
            dimension_semantics=("arbitrary", "arbitrary"), vmem_limit_bytes=_VMEM_LIMIT),
        name="merge",
    )(x, cbz, attn, qm, km, vm, w["p_conv"], w["p_attn"], w["p_mem"], w["w_g"], w["b_gate"],
      w["w_out"], w["mix_pre"], w["mix_post"])


def _rope_tables(t):
    pos = jnp.arange(t)
    row = (pos // GRID_W).astype(F32)
    col = (pos % GRID_W).astype(F32)
    inv = 1.0 / (ROPE_THETA ** (jnp.arange(0, AXIS_DIM, 2, dtype=F32) / AXIS_DIM))
    ang = jnp.concatenate([row[:, None] * inv, col[:, None] * inv], axis=-1)
    cos = jnp.repeat(jnp.cos(ang), 2, axis=-1)
    sin = jnp.stack([-jnp.sin(ang), jnp.sin(ang)], axis=-1).reshape(t, HEAD_DIM)
    reps = LANES // HEAD_DIM
    return jnp.tile(cos, (1, reps)), jnp.tile(sin, (1, reps))


def _head_sum_matrix(width):
    head = jnp.arange(width) // HEAD_DIM
    return (head[:, None] == head[None, :]).astype(BF16)


def _swiglu_weights(w1, w3):
    n = D_FF // FFN_CK
    a = w1.astype(BF16).reshape(D_MODEL, n, FFN_CK)
    b = w3.astype(BF16).reshape(D_MODEL, n, FFN_CK)
    return jnp.concatenate([a, b], axis=-1).transpose(1, 0, 2)


def _layer(x, mem, w, tables):
    b, t, d = x.shape
    x1, u = _ffn(x.reshape(b * t, d), w["ffn1_pre"], w["ffn1_w13"], w["ffn1_w2"], w["ffn1_post"],
                 w["mix_pre"], emit_u=True)
    cbz, q, kt, vt, qm, km, vm = _inproj(u.reshape(b, t, d), mem, w, tables)
    attn = _attn(q, kt, vt)
    x2 = _merge(x1.reshape(b, t, d), cbz, attn, qm, km, vm, w)
    (y,) = _ffn(x2.reshape(b * t, d), w["ffn2_pre"], w["ffn2_w13"], w["ffn2_w2"], w["ffn2_post"],
                w["mix_pre"], emit_u=False)
    return y.reshape(b, t, d)


def kernel(x_prompt, x_sample, mem_prompt, mem_sample, ffn1_pre, ffn1_w1, ffn1_w3, ffn1_w2, ffn1_post, mix_pre, w_in, conv_w, conv_b, p_conv, q_norm, k_norm, p_attn, mem_norm, w_mem_kv, p_mem, b_gate, w_out, mix_post, ffn2_pre, ffn2_w1, ffn2_w3, ffn2_w2, ffn2_post):
    y_prompt, y_sample = x_prompt, x_sample
    for l in range(ffn1_pre.shape[0]):
        win = w_in[l].astype(BF16)
        o_q = 3 * D_CONV
        o_k = o_q + N_HEADS * HEAD_DIM
        o_qm = o_k + 2 * GROUP_W
        o_g = o_qm + D_MEM
        w = {
            "ffn1_pre": ffn1_pre[l][None], "ffn1_post": ffn1_post[l][None],
            "ffn1_w13": _swiglu_weights(ffn1_w1[l], ffn1_w3[l]), "ffn1_w2": ffn1_w2[l].astype(BF16),
            "ffn2_pre": ffn2_pre[l][None], "ffn2_post": ffn2_post[l][None],
            "ffn2_w13": _swiglu_weights(ffn2_w1[l], ffn2_w3[l]), "ffn2_w2": ffn2_w2[l].astype(BF16),
            "mix_pre": mix_pre[l][None], "mix_post": mix_post[l][None],
            "w_c": win[:, :o_q], "w_q": win[:, o_q:o_k], "w_kv": win[:, o_k:o_qm], "w_qm": win[:, o_qm:o_g],
            "w_g": win[:, o_g:], "b_gate": b_gate[l][None],
            "conv_w": conv_w[l], "conv_b": conv_b[l][None],
            "q_gain": jnp.tile(q_norm[l], N_HEADS)[None], "k_gain": jnp.tile(k_norm[l], N_KV_HEADS)[None],
            "mem_norm": mem_norm[l][None], "w_mkv": w_mem_kv[l].astype(BF16),
            "p_conv": p_conv[l].astype(BF16), "p_attn": p_attn[l].astype(BF16), "p_mem": p_mem[l].astype(BF16),
            "w_out": w_out[l].astype(BF16),
        }
        tables = _rope_tables(y_prompt.shape[1]) + (_head_sum_matrix(N_HEADS * HEAD_DIM), _head_sum_matrix(GROUP_W))
        y_prompt = _layer(y_prompt, mem_prompt, w, tables)
        tables_s = _rope_tables(y_sample.shape[1]) + tables[2:]
        y_sample = _layer(y_sample, mem_sample, w, tables_s)
    return (y_prompt, y_sample)
```

```python
import functools

import jax
import jax.numpy as jnp
from jax import lax
from jax.experimental import pallas as pl
from jax.experimental.pallas import tpu as pltpu

D_MODEL = 1024
D_CONV = 512
N_HEADS = 16
N_KV_HEADS = 4
GROUP = N_HEADS // N_KV_HEADS
HEAD_DIM = 64
AXIS_DIM = HEAD_DIM // 2
ROPE_THETA = 10000.0
GRID_W = 64
N_MEM = 256
N_MEM_HEADS = 4
MEM_HEAD_DIM = 128
N_BRANCH = 3
D_FF = 2816
EPS = 1e-6

LANES = 128
MXU_N = 256
GROUP_W = GROUP * HEAD_DIM
D_MEM = N_MEM_HEADS * MEM_HEAD_DIM

FFN_TM = 512
FFN_CK = MXU_N
ROW_CHUNK = 512
ATTN_TQ = 512
V_ROWS = HEAD_DIM + 16
LOG2_E = 1.4426950408889634

F32 = jnp.float32
BF16 = jnp.bfloat16

_VMEM_LIMIT = 56 * 1024 * 1024


def _resident(shape):
    zeros = (0,) * len(shape)
    return pl.BlockSpec(shape, lambda *_: zeros, pipeline_mode=pl.Buffered(1))


def _rms(x, g):
    ms = jnp.mean(x * x, axis=-1, keepdims=True)
    return x * lax.rsqrt(ms + EPS) * g


def _ffn_kernel(x_ref, pre_ref, w13_ref, w2_ref, post_ref, mixg_ref, o_ref, *rest, emit_u):
    if emit_u:
        u_ref, h_scr = rest
    else:
        (h_scr,) = rest
    x = x_ref[...]
    xn = _rms(x, pre_ref[...]).astype(BF16)
    for c in range(D_FF // FFN_CK):
        r = jnp.dot(xn, w13_ref[:, 2 * c * FFN_CK:2 * (c + 1) * FFN_CK],
                    preferred_element_type=F32)
        a = r[:, :FFN_CK]
        b = r[:, FFN_CK:]
        h = a * (1.0 / (1.0 + jnp.exp(-a))) * b
        h_scr[:, c * FFN_CK:(c + 1) * FFN_CK] = h.astype(BF16)
    y = jnp.dot(h_scr[...], w2_ref[...], preferred_element_type=F32)
    xo = x + 0.5 * _rms(y, post_ref[...])
    o_ref[...] = xo
    if emit_u:
        u_ref[...] = _rms(xo, mixg_ref[...]).astype(BF16)


def _ffn(x2d, pre, w13, w2, post, mixg, emit_u):
    n = x2d.shape[0]
    tm = FFN_TM
    row = lambda i: (i, 0)
    out_shape = [jax.ShapeDtypeStruct((n, D_MODEL), F32)]
    out_specs = [pl.BlockSpec((tm, D_MODEL), row)]
    if emit_u:
        out_shape.append(jax.ShapeDtypeStruct((n, D_MODEL), BF16))
        out_specs.append(pl.BlockSpec((tm, D_MODEL), row))
    res = pl.pallas_call(
        functools.partial(_ffn_kernel, emit_u=emit_u),
        grid=(n // tm,),
        in_specs=[
            pl.BlockSpec((tm, D_MODEL), row),
            _resident((1, D_MODEL)),
            _resident(w13.shape),
            _resident(w2.shape),
            _resident((1, D_MODEL)),
            _resident((1, D_MODEL)),
        ],
        out_specs=out_specs,
        out_shape=out_shape,
        scratch_shapes=[pltpu.VMEM((tm, D_FF), BF16)],
        compiler_params=pltpu.CompilerParams(
            dimension_semantics=("arbitrary",), vmem_limit_bytes=_VMEM_LIMIT),
        name="ffn_u" if emit_u else "ffn",
    )(x2d, pre, w13, w2, post, mixg)
    return res


def _pair_swap(x):
    lane = lax.broadcasted_iota(jnp.int32, x.shape, 1)
    nxt = pltpu.roll(x, LANES - 1, 1)
    prv = pltpu.roll(x, 1, 1)
    return jnp.where((lane & 1) == 0, nxt, prv)


def _norm_rope(r, ssum_ref, g_ref, cos, sin, scale):
    ss = jnp.dot((r * r).astype(BF16), ssum_ref[...], preferred_element_type=F32)
    rinv = lax.rsqrt(ss * (1.0 / HEAD_DIM) + EPS) * scale
    xg = r * g_ref[...]
    outs = []
    for j in range(r.shape[1] // LANES):
        xs = xg[:, j * LANES:(j + 1) * LANES]
        outs.append(xs * cos + _pair_swap(xs) * sin)
    return jnp.concatenate(outs, axis=1) * rinv


def _group_tile(x, g):
    pair = x[:, (g // 2) * LANES:(g // 2 + 1) * LANES]
    rot = pltpu.roll(pair, HEAD_DIM, 1)
    lane = lax.broadcasted_iota(jnp.int32, pair.shape, 1)
    lo = lane < HEAD_DIM
    t = jnp.where(lo, pair, rot) if g % 2 == 0 else jnp.where(lo, rot, pair)
    return jnp.concatenate([t, t], axis=1)


def _inproj_kernel(u_ref, mem_ref, wc_ref, wq_ref, wkv_ref, wqm_ref, wmkv_ref,
                   convw_ref, convb_ref, qg_ref, kg_ref, memg_ref, cos_ref, sin_ref, sq_ref, sk_ref,
                   cbz_ref, q_ref, kt_ref, vt_ref, qm_ref, km_ref, vm_ref,
                   z_scr, cb_scr):
    t_len = cbz_ref.shape[0]
    n_chunks = t_len // ROW_CHUNK
    j = pl.program_id(1)

    @pl.when(j == 0)
    def _():
        mn = _rms(mem_ref[...], memg_ref[...]).astype(BF16)
        kv = jnp.dot(mn, wmkv_ref[...], preferred_element_type=F32)
        km_ref[...] = kv[:, :D_MEM].astype(BF16)
        vm_ref[...] = kv[:, D_MEM:].astype(BF16)
        zero_rows = jnp.zeros((8, D_CONV), F32)
        z_scr[0:8, :] = zero_rows
        z_scr[t_len + 8:t_len + 16, :] = zero_rows

    r0 = pl.multiple_of(j * ROW_CHUNK, ROW_CHUNK)
    uc = u_ref[...]

    rc = jnp.dot(uc, wc_ref[...], preferred_element_type=F32)
    z_scr[pl.ds(r0 + 8, ROW_CHUNK), :] = rc[:, 2 * D_CONV:] * rc[:, :D_CONV]
    cb_scr[pl.ds(r0, ROW_CHUNK), :] = rc[:, D_CONV:2 * D_CONV].astype(BF16)

    cos = cos_ref[pl.ds(r0, ROW_CHUNK), :]
    sin = sin_ref[pl.ds(r0, ROW_CHUNK), :]
    rq = jnp.dot(uc, wq_ref[...], preferred_element_type=F32)
    q_ref[...] = _norm_rope(rq, sq_ref, qg_ref, cos, sin, HEAD_DIM ** -0.5 * LOG2_E).astype(BF16)

    rkv = jnp.dot(uc, wkv_ref[...], preferred_element_type=F32)
    kk = _norm_rope(rkv[:, :GROUP_W], sk_ref, kg_ref, cos, sin, 1.0)
    vvt = rkv[:, GROUP_W:].T
    tail = (lax.broadcasted_iota(jnp.int32, (V_ROWS - HEAD_DIM, ROW_CHUNK), 0) == 0).astype(BF16)
    for g in range(N_KV_HEADS):
        kt_ref[g] = _group_tile(kk, g).astype(BF16)
        vt_ref[g, 0:HEAD_DIM, :] = vvt[g * HEAD_DIM:(g + 1) * HEAD_DIM, :].astype(BF16)
        vt_ref[g, HEAD_DIM:V_ROWS, :] = tail

    rqm = jnp.dot(uc, wqm_ref[...], preferred_element_type=F32)
    qm_ref[...] = (rqm * (MEM_HEAD_DIM ** -0.5)).astype(BF16)

    @pl.when(j == n_chunks - 1)
    def _():
        cw = convw_ref[...]
        for c in range(n_chunks):
            c0 = c * ROW_CHUNK
            zc = (z_scr[c0 + 7:c0 + 7 + ROW_CHUNK, :] * cw[0:1, :]
                  + z_scr[c0 + 8:c0 + 8 + ROW_CHUNK, :] * cw[1:2, :]
                  + z_scr[c0 + 9:c0 + 9 + ROW_CHUNK, :] * cw[2:3, :]
                  + convb_ref[...])
            cbz_ref[c0:c0 + ROW_CHUNK, :] = (cb_scr[c0:c0 + ROW_CHUNK, :].astype(F32) * zc).astype(BF16)


def _inproj(u, mem, w, tables):
    b, t, _ = u.shape
    cos, sin, sq, sk = tables
    rows = lambda width: pl.BlockSpec((None, ROW_CHUNK, width), lambda i, j: (i, j, 0))
    tiled = pl.BlockSpec((None, N_KV_HEADS, ROW_CHUNK, GROUP_W), lambda i, j: (i, 0, j, 0))
    vspec = pl.BlockSpec((None, N_KV_HEADS, V_ROWS, ROW_CHUNK), lambda i, j: (i, 0, 0, j))
    memspec = pl.BlockSpec((None, N_MEM, D_MEM), lambda i, j: (i, 0, 0))
    return pl.pallas_call(
        _inproj_kernel,
        grid=(b, t // ROW_CHUNK),
        in_specs=[
            rows(D_MODEL),
            pl.BlockSpec((None, N_MEM, D_MODEL), lambda i, j: (i, 0, 0)),
            _resident(w["w_c"].shape), _resident(w["w_q"].shape), _resident(w["w_kv"].shape),
            _resident(w["w_qm"].shape), _resident(w["w_mkv"].shape),
            _resident((3, D_CONV)), _resident((1, D_CONV)),
            _resident((1, N_HEADS * HEAD_DIM)), _resident((1, GROUP_W)), _resident((1, D_MODEL)),
            _resident(cos.shape), _resident(sin.shape), _resident(sq.shape), _resident(sk.shape),
        ],
        out_specs=[pl.BlockSpec((None, t, D_CONV), lambda i, j: (i, 0, 0)), rows(N_HEADS * HEAD_DIM),
                   tiled, vspec, rows(D_MEM), memspec, memspec],
        out_shape=[
            jax.ShapeDtypeStruct((b, t, D_CONV), BF16),
            jax.ShapeDtypeStruct((b, t, N_HEADS * HEAD_DIM), BF16),
            jax.ShapeDtypeStruct((b, N_KV_HEADS, t, GROUP_W), BF16),
            jax.ShapeDtypeStruct((b, N_KV_HEADS, V_ROWS, t), BF16),
            jax.ShapeDtypeStruct((b, t, D_MEM), BF16),
            jax.ShapeDtypeStruct((b, N_MEM, D_MEM), BF16),
            jax.ShapeDtypeStruct((b, N_MEM, D_MEM), BF16),
        ],
        scratch_shapes=[pltpu.VMEM((t + 16, D_CONV), F32), pltpu.VMEM((t, D_CONV), BF16)],
        compiler_params=pltpu.CompilerParams(
            dimension_semantics=("arbitrary", "arbitrary"), vmem_limit_bytes=_VMEM_LIMIT),
        name="inproj",
    )(u, mem, w["w_c"], w["w_q"], w["w_kv"], w["w_qm"], w["w_mkv"],
      w["conv_w"], w["conv_b"], w["q_gain"], w["k_gain"], w["mem_norm"], cos, sin, sq, sk)


def _attn_step(q, kt_ref, vt_ref, s_new, m_new, s_cur, m_cur):
    tq = q.shape[0]
    t, cols = s_new.shape
    qf = q.astype(F32)
    head = lax.broadcasted_iota(jnp.int32, qf.shape, 1) // HEAD_DIM
    qs = jnp.concatenate([jnp.where(head == h, qf, 0.0) for h in range(GROUP)], axis=0).astype(BF16)
    st = lax.dot_general(kt_ref[...], qs, (((1,), (1,)), ((), ())), preferred_element_type=F32)
    s_new[...] = st
    m_new[...] = jnp.max(st, axis=0, keepdims=True)

    m = m_cur[...]
    accs = []
    for c in range(cols // MXU_N):
        csl = slice(c * MXU_N, (c + 1) * MXU_N)
        acc = None
        for k in range(t // MXU_N):
            ksl = slice(k * MXU_N, (k + 1) * MXU_N)
            p = jnp.exp2(s_cur[ksl, csl] - m[:, csl]).astype(BF16)
            d = jnp.dot(vt_ref[:, ksl], p, preferred_element_type=F32)
            acc = d if acc is None else acc + d
        accs.append(acc)
    ot = jnp.concatenate(accs, axis=1)
    on = ot[0:HEAD_DIM, :] * (1.0 / ot[HEAD_DIM:HEAD_DIM + 1, :])
    out_t = jnp.concatenate([on[:, h * tq:(h + 1) * tq] for h in range(GROUP)], axis=0)
    return out_t.T.astype(BF16)


def _attn_kernel(q_ref, kt_ref, vt_ref, o_ref, s0, s1, m0, m1):
    n = pl.program_id(0)

    @pl.when(n == 0)
    def _():
        s1[...] = jnp.zeros_like(s1)
        m1[...] = jnp.zeros_like(m1)

    @pl.when(n % 2 == 0)
    def _():
        o_ref[...] = _attn_step(q_ref[...], kt_ref, vt_ref, s0, m0, s1, m1)

    @pl.when(n % 2 == 1)
    def _():
        o_ref[...] = _attn_step(q_ref[...], kt_ref, vt_ref, s1, m1, s0, m0)


def _attn(q, kt, vt):
    b, t, _ = q.shape
    tq = ATTN_TQ
    tiles_per_seq = t // tq
    n_tiles = b * N_KV_HEADS * tiles_per_seq

    def place(i):
        return i // (N_KV_HEADS * tiles_per_seq), (i // tiles_per_seq) % N_KV_HEADS, i % tiles_per_seq

    def score_tile(n):
        return place(jnp.minimum(n, n_tiles - 1))

    def value_tile(n):
        return place(jnp.maximum(n - 1, 0))

    def q_map(n):
        bi, g, j = score_tile(n)
        return bi, j, g

    def kt_map(n):
        bi, g, _ = score_tile(n)
        return bi, g, 0, 0

    def vt_map(n):
        bi, g, _ = value_tile(n)
        return bi, g, 0, 0

    def o_map(n):
        bi, g, j = value_tile(n)
        return bi, j, g

    cols = GROUP * tq
    return pl.pallas_call(
        _attn_kernel,
        grid=(n_tiles + 1,),
        in_specs=[pl.BlockSpec((None, tq, GROUP_W), q_map),
                  pl.BlockSpec((None, None, t, GROUP_W), kt_map),
                  pl.BlockSpec((None, None, V_ROWS, t), vt_map)],
        out_specs=pl.BlockSpec((None, tq, GROUP_W), o_map),
        out_shape=jax.ShapeDtypeStruct((b, t, N_HEADS * HEAD_DIM), BF16),
        scratch_shapes=[pltpu.VMEM((t, cols), F32), pltpu.VMEM((t, cols), F32),
                        pltpu.VMEM((1, cols), F32), pltpu.VMEM((1, cols), F32)],
        compiler_params=pltpu.CompilerParams(
            dimension_semantics=("arbitrary",), vmem_limit_bytes=_VMEM_LIMIT),
        name="attn",
    )(q, kt, vt)


def _merge_kernel(x_ref, cbz_ref, attn_ref, qm_ref, km_ref, vm_ref,
                  pconv_ref, pattn_ref, pmem_ref, wg_ref, bg_ref, wout_ref, mixg_ref, postg_ref, o_ref):
    x = x_ref[...]
    u = _rms(x, mixg_ref[...]).astype(BF16)

    qm = qm_ref[...]
    heads = []
    for h in range(N_MEM_HEADS):
        cols = slice(h * MEM_HEAD_DIM, (h + 1) * MEM_HEAD_DIM)
        s = lax.dot_general(qm[:, cols], km_ref[:, cols], (((1,), (1,)), ((), ())),
                            preferred_element_type=F32)
        p = jnp.exp(s - jnp.max(s, axis=-1, keepdims=True))
        l = jnp.sum(p, axis=-1, keepdims=True)
        oh = jnp.dot(p.astype(BF16), vm_ref[:, cols], preferred_element_type=F32)
        heads.append(oh * (1.0 / l))
    memo = jnp.concatenate(heads, axis=1).astype(BF16)

    branches = (
        (cbz_ref[...], pconv_ref),
        (attn_ref[...], pattn_ref),
        (memo, pmem_ref),
    )
    merged = None
    for br, (act, proj_ref) in enumerate(branches):
        cols = slice(br * D_MODEL, (br + 1) * D_MODEL)
        gate = jnp.dot(u, wg_ref[:, cols], preferred_element_type=F32) + bg_ref[:, cols]
        gate = 1.0 / (1.0 + jnp.exp(-gate))
        term = gate * jnp.dot(act, proj_ref[...], preferred_element_type=F32)
        merged = term if merged is None else merged + term
    y = jnp.dot(merged.astype(BF16), wout_ref[...], preferred_element_type=F32)
    o_ref[...] = x + _rms(y, postg_ref[...])


def _merge(x, cbz, attn, qm, km, vm, w):
    b, t, _ = x.shape
    tm = FFN_TM
    tok = lambda width: pl.BlockSpec((None, tm, width), lambda i, j: (i, j, 0))
    memspec = pl.BlockSpec((None, N_MEM, D_MEM), lambda i, j: (i, 0, 0))
    return pl.pallas_call(
        _merge_kernel,
        grid=(b, t // tm),
        in_specs=[
            tok(D_MODEL), tok(D_CONV), tok(N_HEADS * HEAD_DIM), tok(D_MEM), memspec, memspec,
            _resident(w["p_conv"].shape), _resident(w["p_attn"].shape), _resident(w["p_mem"].shape),
            _resident(w["w_g"].shape), _resident((1, N_BRANCH * D_MODEL)), _resident(w["w_out"].shape),
            _resident((1, D_MODEL)), _resident((1, D_MODEL)),
        ],
        out_specs=tok(D_MODEL),
        out_shape=jax.ShapeDtypeStruct((b, t, D_MODEL), F32),
        compiler_params=pltpu.CompilerParams(
            dimension_semantics=("arbitrary", "arbitrary"), vmem_limit_bytes=_VMEM_LIMIT),
        name="merge",
    )(x, cbz, attn, qm, km, vm, w["p_conv"], w["p_attn"], w["p_mem"], w["w_g"], w["b_gate"],
      w["w_out"], w["mix_pre"], w["mix_post"])


def _rope_tables(t):
    pos = jnp.arange(t)
    row = (pos // GRID_W).astype(F32)
    col = (pos % GRID_W).astype(F32)
    inv = 1.0 / (ROPE_THETA ** (jnp.arange(0, AXIS_DIM, 2, dtype=F32) / AXIS_DIM))
    ang = jnp.concatenate([row[:, None] * inv, col[:, None] * inv], axis=-1)
    cos = jnp.repeat(jnp.cos(ang), 2, axis=-1)
    sin = jnp.stack([-jnp.sin(ang), jnp.sin(ang)], axis=-1).reshape(t, HEAD_DIM)
    reps = LANES // HEAD_DIM
    return jnp.tile(cos, (1, reps)), jnp.tile(sin, (1, reps))


def _head_sum_matrix(width):
    head = jnp.arange(width) // HEAD_DIM
    return (head[:, None] == head[None, :]).astype(BF16)


def _swiglu_weights(w1, w3):
    n = D_FF // FFN_CK
    a = w1.astype(BF16).reshape(D_MODEL, n, FFN_CK)
    b = w3.astype(BF16).reshape(D_MODEL, n, FFN_CK)
    return jnp.concatenate([a, b], axis=-1).reshape(D_MODEL, 2 * D_FF)


def _layer(x, mem, w, tables):
    b, t, d = x.shape
    x1, u = _ffn(x.reshape(b * t, d), w["ffn1_pre"], w["ffn1_w13"], w["ffn1_w2"], w["ffn1_post"],
                 w["mix_pre"], emit_u=True)
    cbz, q, kt, vt, qm, km, vm = _inproj(u.reshape(b, t, d), mem, w, tables)
    attn = _attn(q, kt, vt)
    x2 = _merge(x1.reshape(b, t, d), cbz, attn, qm, km, vm, w)
    (y,) = _ffn(x2.reshape(b * t, d), w["ffn2_pre"], w["ffn2_w13"], w["ffn2_w2"], w["ffn2_post"],
                w["mix_pre"], emit_u=False)
    return y.reshape(b, t, d)


def kernel(x_prompt, x_sample, mem_prompt, mem_sample, ffn1_pre, ffn1_w1, ffn1_w3, ffn1_w2, ffn1_post, mix_pre, w_in, conv_w, conv_b, p_conv, q_norm, k_norm, p_attn, mem_norm, w_mem_kv, p_mem, b_gate, w_out, mix_post, ffn2_pre, ffn2_w1, ffn2_w3, ffn2_w2, ffn2_post):
    y_prompt, y_sample = x_prompt, x_sample
    for l in range(ffn1_pre.shape[0]):
        win = w_in[l].astype(BF16)
        o_q = 3 * D_CONV
        o_k = o_q + N_HEADS * HEAD_DIM
        o_qm = o_k + 2 * GROUP_W
        o_g = o_qm + D_MEM
        w = {
            "ffn1_pre": ffn1_pre[l][None], "ffn1_post": ffn1_post[l][None],
            "ffn1_w13": _swiglu_weights(ffn1_w1[l], ffn1_w3[l]), "ffn1_w2": ffn1_w2[l].astype(BF16),
            "ffn2_pre": ffn2_pre[l][None], "ffn2_post": ffn2_post[l][None],
            "ffn2_w13": _swiglu_weights(ffn2_w1[l], ffn2_w3[l]), "ffn2_w2": ffn2_w2[l].astype(BF16),
            "mix_pre": mix_pre[l][None], "mix_post": mix_post[l][None],
            "w_c": win[:, :o_q], "w_q": win[:, o_q:o_k], "w_kv": win[:, o_k:o_qm], "w_qm": win[:, o_qm:o_g],
            "w_g": win[:, o_g:], "b_gate": b_gate[l][None],
            "conv_w": conv_w[l], "conv_b": conv_b[l][None],
            "q_gain": jnp.tile(q_norm[l], N_HEADS)[None], "k_gain": jnp.tile(k_norm[l], N_KV_HEADS)[None],
            "mem_norm": mem_norm[l][None], "w_mkv": w_mem_kv[l].astype(BF16),
            "p_conv": p_conv[l].astype(BF16), "p_attn": p_attn[l].astype(BF16), "p_mem": p_mem[l].astype(BF16),
            "w_out": w_out[l].astype(BF16),
        }
        tables = _rope_tables(y_prompt.shape[1]) + (_head_sum_matrix(N_HEADS * HEAD_DIM), _head_sum_matrix(GROUP_W))
        y_prompt = _layer(y_prompt, mem_prompt, w, tables)
        tables_s = _rope_tables(y_sample.shape[1]) + tables[2:]
        y_sample = _layer(y_sample, mem_sample, w, tables_s)
    return (y_prompt, y_sample)
```

```python
import functools

import jax
import jax.numpy as jnp
import numpy as np
from jax import lax
from jax.experimental import pallas as pl
from jax.experimental.pallas import tpu as pltpu

D_MODEL = 1024
D_CONV = 512
N_HEADS = 16
N_KV_HEADS = 4
GROUP = N_HEADS // N_KV_HEADS
HEAD_DIM = 64
AXIS_DIM = HEAD_DIM // 2
ROPE_THETA = 10000.0
GRID_W = 64
N_MEM = 256
N_MEM_HEADS = 4
MEM_HEAD_DIM = 128
N_BRANCH = 3
D_FF = 2816
EPS = 1e-6

LANES = 128
MXU_N = 256
GROUP_W = GROUP * HEAD_DIM
D_MEM = N_MEM_HEADS * MEM_HEAD_DIM
W_IN_Q = 3 * D_CONV
W_IN_K = W_IN_Q + N_HEADS * HEAD_DIM
W_IN_QM = W_IN_K + 2 * N_KV_HEADS * HEAD_DIM
W_IN_G = W_IN_QM + D_MEM

FFN_TM = 1024
MERGE_TM = 512
FFN_CK = MXU_N
ROW_CHUNK = 512
ATTN_TQ = 512
V_ROWS = HEAD_DIM + 16
LOG2_E = 1.4426950408889634

F32 = jnp.float32
BF16 = jnp.bfloat16

_VMEM_LIMIT = 56 * 1024 * 1024


def _resident(shape):
    zeros = (0,) * len(shape)
    return pl.BlockSpec(shape, lambda *_: zeros, pipeline_mode=pl.Buffered(1))


def _rms(x, g):
    ms = jnp.mean(x * x, axis=-1, keepdims=True)
    return x * lax.rsqrt(ms + EPS) * g


def _ffn_kernel(x_ref, pre_ref, w13_ref, w2_ref, post_ref, mixg_ref, o_ref, *rest, emit_u):
    if emit_u:
        u_ref, h_scr = rest
    else:
        (h_scr,) = rest
    x = x_ref[...]
    xn = _rms(x, pre_ref[...]).astype(BF16)
    for c in range(D_FF // FFN_CK):
        r = jnp.dot(xn, w13_ref[:, 2 * c * FFN_CK:2 * (c + 1) * FFN_CK],
                    preferred_element_type=F32)
        a = r[:, :FFN_CK]
        b = r[:, FFN_CK:]
        h = a * (1.0 / (1.0 + jnp.exp(-a))) * b
        h_scr[:, c * FFN_CK:(c + 1) * FFN_CK] = h.astype(BF16)
    y = jnp.dot(h_scr[...], w2_ref[...], preferred_element_type=F32)
    xo = x + 0.5 * _rms(y, post_ref[...])
    o_ref[...] = xo
    if emit_u:
        u_ref[...] = _rms(xo, mixg_ref[...]).astype(BF16)


def _ffn(x2d, pre, w13, w2, post, mixg, emit_u):
    n = x2d.shape[0]
    tm = FFN_TM
    row = lambda i: (i, 0)
    out_shape = [jax.ShapeDtypeStruct((n, D_MODEL), F32)]
    out_specs = [pl.BlockSpec((tm, D_MODEL), row)]
    if emit_u:
        out_shape.append(jax.ShapeDtypeStruct((n, D_MODEL), BF16))
        out_specs.append(pl.BlockSpec((tm, D_MODEL), row))
    res = pl.pallas_call(
        functools.partial(_ffn_kernel, emit_u=emit_u),
        grid=(n // tm,),
        in_specs=[
            pl.BlockSpec((tm, D_MODEL), row),
            _resident((1, D_MODEL)),
            _resident(w13.shape),
            _resident(w2.shape),
            _resident((1, D_MODEL)),
            _resident((1, D_MODEL)),
        ],
        out_specs=out_specs,
        out_shape=out_shape,
        scratch_shapes=[pltpu.VMEM((tm, D_FF), BF16)],
        compiler_params=pltpu.CompilerParams(
            dimension_semantics=("arbitrary",), vmem_limit_bytes=_VMEM_LIMIT),
        name="ffn_u" if emit_u else "ffn",
    )(x2d, pre, w13, w2, post, mixg)
    return res


def _pair_swap(x):
    lane = lax.broadcasted_iota(jnp.int32, x.shape, 1)
    nxt = pltpu.roll(x, LANES - 1, 1)
    prv = pltpu.roll(x, 1, 1)
    return jnp.where((lane & 1) == 0, nxt, prv)


def _norm_rope(r, ssum_ref, g_ref, cos, sin, scale):
    r2 = (r * r).astype(BF16)
    ss = jnp.concatenate(
        [jnp.dot(r2[:, j:j + GROUP_W], ssum_ref[...], preferred_element_type=F32)
         for j in range(0, r.shape[1], GROUP_W)], axis=1)
    rinv = lax.rsqrt(ss * (1.0 / HEAD_DIM) + EPS) * scale
    xg = r * g_ref[...]
    outs = []
    for j in range(r.shape[1] // LANES):
        xs = xg[:, j * LANES:(j + 1) * LANES]
        outs.append(xs * cos + _pair_swap(xs) * sin)
    return jnp.concatenate(outs, axis=1) * rinv


def _group_tile(x, g):
    pair = x[:, (g // 2) * LANES:(g // 2 + 1) * LANES]
    rot = pltpu.roll(pair, HEAD_DIM, 1)
    lane = lax.broadcasted_iota(jnp.int32, pair.shape, 1)
    lo = lane < HEAD_DIM
    t = jnp.where(lo, pair, rot) if g % 2 == 0 else jnp.where(lo, rot, pair)
    return jnp.concatenate([t, t], axis=1)


def _inproj_kernel(u_ref, mem_ref, win_ref, wmkv_ref,
                   convw_ref, convb_ref, qg_ref, kg_ref, memg_ref, cos_ref, sin_ref, hsum_ref,
                   cbz_ref, q_ref, kt_ref, vt_ref, qm_ref, km_ref, vm_ref,
                   z_scr, cb_scr):
    wc_ref = win_ref.at[:, W_IN_Q - 3 * D_CONV:W_IN_Q]
    wq_ref = win_ref.at[:, W_IN_Q:W_IN_K]
    wkv_ref = win_ref.at[:, W_IN_K:W_IN_QM]
    wqm_ref = win_ref.at[:, W_IN_QM:W_IN_G]
    t_len = cbz_ref.shape[0]
    n_chunks = t_len // ROW_CHUNK
    j = pl.program_id(1)

    @pl.when(j == 0)
    def _():
        mn = _rms(mem_ref[...], memg_ref[...]).astype(BF16)
        kv = jnp.dot(mn, wmkv_ref[...], preferred_element_type=F32)
        km_ref[...] = kv[:, :D_MEM].astype(BF16)
        vm_ref[...] = kv[:, D_MEM:].astype(BF16)
        zero_rows = jnp.zeros((8, D_CONV), F32)
        z_scr[0:8, :] = zero_rows
        z_scr[t_len + 8:t_len + 16, :] = zero_rows

    r0 = pl.multiple_of(j * ROW_CHUNK, ROW_CHUNK)
    uc = u_ref[...]

    rc = jnp.dot(uc, wc_ref[...], preferred_element_type=F32)
    z_scr[pl.ds(r0 + 8, ROW_CHUNK), :] = rc[:, 2 * D_CONV:] * rc[:, :D_CONV]
    cb_scr[pl.ds(r0, ROW_CHUNK), :] = rc[:, D_CONV:2 * D_CONV].astype(BF16)

    cos = cos_ref[pl.ds(r0, ROW_CHUNK), :]
    sin = sin_ref[pl.ds(r0, ROW_CHUNK), :]
    rq = jnp.dot(uc, wq_ref[...], preferred_element_type=F32)
    q_ref[...] = _norm_rope(rq, hsum_ref, qg_ref, cos, sin, HEAD_DIM ** -0.5 * LOG2_E).astype(BF16)

    rkv = jnp.dot(uc, wkv_ref[...], preferred_element_type=F32)
    kk = _norm_rope(rkv[:, :GROUP_W], hsum_ref, kg_ref, cos, sin, 1.0)
    vvt = rkv[:, GROUP_W:].T
    tail = (lax.broadcasted_iota(jnp.int32, (V_ROWS - HEAD_DIM, ROW_CHUNK), 0) == 0).astype(BF16)
    for g in range(N_KV_HEADS):
        kt_ref[g] = _group_tile(kk, g).astype(BF16)
        vt_ref[g, 0:HEAD_DIM, :] = vvt[g * HEAD_DIM:(g + 1) * HEAD_DIM, :].astype(BF16)
        vt_ref[g, HEAD_DIM:V_ROWS, :] = tail

    rqm = jnp.dot(uc, wqm_ref[...], preferred_element_type=F32)
    qm_ref[...] = (rqm * (MEM_HEAD_DIM ** -0.5)).astype(BF16)

    @pl.when(j == n_chunks - 1)
    def _():
        cw = convw_ref[...]
        for c in range(n_chunks):
            c0 = c * ROW_CHUNK
            zc = (z_scr[c0 + 7:c0 + 7 + ROW_CHUNK, :] * cw[0:1, :]
                  + z_scr[c0 + 8:c0 + 8 + ROW_CHUNK, :] * cw[1:2, :]
                  + z_scr[c0 + 9:c0 + 9 + ROW_CHUNK, :] * cw[2:3, :]
                  + convb_ref[...])
            cbz_ref[c0:c0 + ROW_CHUNK, :] = (cb_scr[c0:c0 + ROW_CHUNK, :].astype(F32) * zc).astype(BF16)


def _inproj(u, mem, w, tables):
    b, t, _ = u.shape
    cos, sin, hsum = tables
    rows = lambda width: pl.BlockSpec((None, ROW_CHUNK, width), lambda i, j: (i, j, 0))
    tiled = pl.BlockSpec((None, N_KV_HEADS, ROW_CHUNK, GROUP_W), lambda i, j: (i, 0, j, 0))
    vspec = pl.BlockSpec((None, N_KV_HEADS, V_ROWS, ROW_CHUNK), lambda i, j: (i, 0, 0, j))
    memspec = pl.BlockSpec((None, N_MEM, D_MEM), lambda i, j: (i, 0, 0))
    return pl.pallas_call(
        _inproj_kernel,
        grid=(b, t // ROW_CHUNK),
        in_specs=[
            rows(D_MODEL),
            pl.BlockSpec((None, N_MEM, D_MODEL), lambda i, j: (i, 0, 0)),
            _resident(w["w_in"].shape), _resident(w["w_mkv"].shape),
            _resident((3, D_CONV)), _resident((1, D_CONV)),
            _resident((1, N_HEADS * HEAD_DIM)), _resident((1, GROUP_W)), _resident((1, D_MODEL)),
            _resident(cos.shape), _resident(sin.shape), _resident(hsum.shape),
        ],
        out_specs=[pl.BlockSpec((None, t, D_CONV), lambda i, j: (i, 0, 0)), rows(N_HEADS * HEAD_DIM),
                   tiled, vspec, rows(D_MEM), memspec, memspec],
        out_shape=[
            jax.ShapeDtypeStruct((b, t, D_CONV), BF16),
            jax.ShapeDtypeStruct((b, t, N_HEADS * HEAD_DIM), BF16),
            jax.ShapeDtypeStruct((b, N_KV_HEADS, t, GROUP_W), BF16),
            jax.ShapeDtypeStruct((b, N_KV_HEADS, V_ROWS, t), BF16),
            jax.ShapeDtypeStruct((b, t, D_MEM), BF16),
            jax.ShapeDtypeStruct((b, N_MEM, D_MEM), BF16),
            jax.ShapeDtypeStruct((b, N_MEM, D_MEM), BF16),
        ],
        scratch_shapes=[pltpu.VMEM((t + 16, D_CONV), F32), pltpu.VMEM((t, D_CONV), BF16)],
        compiler_params=pltpu.CompilerParams(
            dimension_semantics=("arbitrary", "arbitrary"), vmem_limit_bytes=_VMEM_LIMIT),
        name="inproj",
    )(u, mem, w["w_in"], w["w_mkv"],
      w["conv_w"], w["conv_b"], w["q_gain"], w["k_gain"], w["mem_norm"], cos, sin, hsum)


def _attn_step(q, kt_ref, vt_ref, s_new, m_new, s_cur, m_cur):
    tq = q.shape[0]
    t = s_new.shape[0]
    cols = GROUP * tq
    qf = q.astype(F32)
    head = lax.broadcasted_iota(jnp.int32, qf.shape, 1) // HEAD_DIM
    qs = jnp.concatenate([jnp.where(head == h, qf, 0.0) for h in range(GROUP)], axis=0).astype(BF16)
    st = lax.dot_general(kt_ref[...], qs, (((1,), (1,)), ((), ())), preferred_element_type=F32)
    s_new[:, 0:cols] = st
    m_new[...] = jnp.max(st, axis=0, keepdims=True)

    m = m_cur[...]
    accs = []
    for c in range(cols // MXU_N):
        csl = slice(c * MXU_N, (c + 1) * MXU_N)
        acc = None
        for k in range(t // MXU_N):
            ksl = slice(k * MXU_N, (k + 1) * MXU_N)
            p = jnp.exp2(s_cur[ksl, csl] - m[:, csl]).astype(BF16)
            d = jnp.dot(vt_ref[:, ksl], p, preferred_element_type=F32)
            acc = d if acc is None else acc + d
        accs.append(acc)
    ot = jnp.concatenate(accs, axis=1)
    on = ot[0:HEAD_DIM, :] * (1.0 / ot[HEAD_DIM:HEAD_DIM + 1, :])
    out_t = jnp.concatenate([on[:, h * tq:(h + 1) * tq] for h in range(GROUP)], axis=0)
    return out_t.T.astype(BF16)


def _attn_kernel(q_ref, kt_ref, vt_ref, o_ref, s0, s1, m0, m1):
    n = pl.program_id(0)

    @pl.when(n == 0)
    def _():
        s1[...] = jnp.zeros_like(s1)
        m1[...] = jnp.zeros_like(m1)

    @pl.when(n % 2 == 0)
    def _():
        o_ref[...] = _attn_step(q_ref[...], kt_ref, vt_ref, s0, m0, s1, m1)

    @pl.when(n % 2 == 1)
    def _():
        o_ref[...] = _attn_step(q_ref[...], kt_ref, vt_ref, s1, m1, s0, m0)


def _attn(q, kt, vt):
    b, t, _ = q.shape
    tq = ATTN_TQ
    tiles_per_seq = t // tq
    n_tiles = b * N_KV_HEADS * tiles_per_seq

    def place(i):
        return i // (N_KV_HEADS * tiles_per_seq), (i // tiles_per_seq) % N_KV_HEADS, i % tiles_per_seq

    def score_tile(n):
        return place(jnp.minimum(n, n_tiles - 1))

    def value_tile(n):
        return place(jnp.maximum(n - 1, 0))

    def q_map(n):
        bi, g, j = score_tile(n)
        return bi, j, g

    def kt_map(n):
        bi, g, _ = score_tile(n)
        return bi, g, 0, 0

    def vt_map(n):
        bi, g, _ = value_tile(n)
        return bi, g, 0, 0

    def o_map(n):
        bi, g, j = value_tile(n)
        return bi, j, g

    cols = GROUP * tq
    return pl.pallas_call(
        _attn_kernel,
        grid=(n_tiles + 1,),
        in_specs=[pl.BlockSpec((None, tq, GROUP_W), q_map),
                  pl.BlockSpec((None, None, t, GROUP_W), kt_map),
                  pl.BlockSpec((None, None, V_ROWS, t), vt_map)],
        out_specs=pl.BlockSpec((None, tq, GROUP_W), o_map),
        out_shape=jax.ShapeDtypeStruct((b, t, N_HEADS * HEAD_DIM), BF16),
        scratch_shapes=[pltpu.VMEM((t, cols + LANES), F32), pltpu.VMEM((t, cols + LANES), F32),
                        pltpu.VMEM((1, cols), F32), pltpu.VMEM((1, cols), F32)],
        compiler_params=pltpu.CompilerParams(
            dimension_semantics=("arbitrary",), vmem_limit_bytes=_VMEM_LIMIT),
        name="attn",
    )(q, kt, vt)


def _merge_kernel(x_ref, cbz_ref, attn_ref, qm_ref, km_ref, vm_ref,
                  pconv_ref, pattn_ref, pmem_ref, win_ref, bg_ref, wout_ref, mixg_ref, postg_ref, o_ref):
    x = x_ref[...]
    u = _rms(x, mixg_ref[...]).astype(BF16)

    qm = qm_ref[...]
    heads = []
    for h in range(N_MEM_HEADS):
        cols = slice(h * MEM_HEAD_DIM, (h + 1) * MEM_HEAD_DIM)
        s = lax.dot_general(qm[:, cols], km_ref[:, cols], (((1,), (1,)), ((), ())),
                            preferred_element_type=F32)
        p = jnp.exp(s - jnp.max(s, axis=-1, keepdims=True))
        l = jnp.sum(p, axis=-1, keepdims=True)
        oh = jnp.dot(p.astype(BF16), vm_ref[:, cols], preferred_element_type=F32)
        heads.append(oh * (1.0 / l))
    memo = jnp.concatenate(heads, axis=1).astype(BF16)

    branches = (
        (cbz_ref[...], pconv_ref),
        (attn_ref[...], pattn_ref),
        (memo, pmem_ref),
    )
    merged = None
    for br, (act, proj_ref) in enumerate(branches):
        cols = slice(br * D_MODEL, (br + 1) * D_MODEL)
        gate = jnp.dot(u, win_ref[:, W_IN_G + br * D_MODEL:W_IN_G + (br + 1) * D_MODEL],
                       preferred_element_type=F32) + bg_ref[:, cols]
        gate = 1.0 / (1.0 + jnp.exp(-gate))
        term = gate * jnp.dot(act, proj_ref[...], preferred_element_type=F32)
        merged = term if merged is None else merged + term
    y = jnp.dot(merged.astype(BF16), wout_ref[...], preferred_element_type=F32)
    o_ref[...] = x + _rms(y, postg_ref[...])


def _merge(x, cbz, attn, qm, km, vm, w):
    b, t, _ = x.shape
    tm = MERGE_TM
    tok = lambda width: pl.BlockSpec((None, tm, width), lambda i, j: (i, j, 0))
    memspec = pl.BlockSpec((None, N_MEM, D_MEM), lambda i, j: (i, 0, 0))
    return pl.pallas_call(
        _merge_kernel,
        grid=(b, t // tm),
        in_specs=[
            tok(D_MODEL), tok(D_CONV), tok(N_HEADS * HEAD_DIM), tok(D_MEM), memspec, memspec,
            _resident(w["p_conv"].shape), _resident(w["p_attn"].shape), _resident(w["p_mem"].shape),
            _resident(w["w_in"].shape), _resident((1, N_BRANCH * D_MODEL)), _resident(w["w_out"].shape),
            _resident((1, D_MODEL)), _resident((1, D_MODEL)),
        ],
        out_specs=tok(D_MODEL),
        out_shape=jax.ShapeDtypeStruct((b, t, D_MODEL), F32),
        compiler_params=pltpu.CompilerParams(
            dimension_semantics=("arbitrary", "arbitrary"), vmem_limit_bytes=_VMEM_LIMIT),
        name="merge",
    )(x, cbz, attn, qm, km, vm, w["p_conv"], w["p_attn"], w["p_mem"], w["w_in"], w["b_gate"],
      w["w_out"], w["mix_pre"], w["mix_post"])


def _rope_tables(t):
    pos = jnp.arange(t)
    row = (pos // GRID_W).astype(F32)
    col = (pos % GRID_W).astype(F32)
    inv = 1.0 / (ROPE_THETA ** (jnp.arange(0, AXIS_DIM, 2, dtype=F32) / AXIS_DIM))
    ang = jnp.concatenate([row[:, None] * inv, col[:, None] * inv], axis=-1)
    cos = jnp.repeat(jnp.cos(ang), 2, axis=-1)
    sin = jnp.stack([-jnp.sin(ang), jnp.sin(ang)], axis=-1).reshape(t, HEAD_DIM)
    reps = LANES // HEAD_DIM
    return jnp.tile(cos, (1, reps)), jnp.tile(sin, (1, reps))


def _head_sum_matrix(width):
    head = np.arange(width) // HEAD_DIM
    return jnp.asarray(head[:, None] == head[None, :], dtype=BF16)


def _swiglu_weights(w1, w3):
    n = D_FF // FFN_CK
    a = w1.astype(BF16).reshape(D_MODEL, n, FFN_CK)
    b = w3.astype(BF16).reshape(D_MODEL, n, FFN_CK)
    return jnp.concatenate([a, b], axis=-1).reshape(D_MODEL, 2 * D_FF)


def _layer(x, mem, w, tables):
    b, t, d = x.shape
    x1, u = _ffn(x.reshape(b * t, d), w["ffn1_pre"], w["ffn1_w13"], w["ffn1_w2"], w["ffn1_post"],
                 w["mix_pre"], emit_u=True)
    cbz, q, kt, vt, qm, km, vm = _inproj(u.reshape(b, t, d), mem, w, tables)
    attn = _attn(q, kt, vt)
    x2 = _merge(x1.reshape(b, t, d), cbz, attn, qm, km, vm, w)
    (y,) = _ffn(x2.reshape(b * t, d), w["ffn2_pre"], w["ffn2_w13"], w["ffn2_w2"], w["ffn2_post"],
                w["mix_pre"], emit_u=False)
    return y.reshape(b, t, d)


def kernel(x_prompt, x_sample, mem_prompt, mem_sample, ffn1_pre, ffn1_w1, ffn1_w3, ffn1_w2, ffn1_post, mix_pre, w_in, conv_w, conv_b, p_conv, q_norm, k_norm, p_attn, mem_norm, w_mem_kv, p_mem, b_gate, w_out, mix_post, ffn2_pre, ffn2_w1, ffn2_w3, ffn2_w2, ffn2_post):
    y_prompt, y_sample = x_prompt, x_sample
    for l in range(ffn1_pre.shape[0]):
        w = {
            "ffn1_pre": ffn1_pre[l][None], "ffn1_post": ffn1_post[l][None],
            "ffn1_w13": _swiglu_weights(ffn1_w1[l], ffn1_w3[l]), "ffn1_w2": ffn1_w2[l].astype(BF16),
            "ffn2_pre": ffn2_pre[l][None], "ffn2_post": ffn2_post[l][None],
            "ffn2_w13": _swiglu_weights(ffn2_w1[l], ffn2_w3[l]), "ffn2_w2": ffn2_w2[l].astype(BF16),
            "mix_pre": mix_pre[l][None], "mix_post": mix_post[l][None],
            "w_in": w_in[l].astype(BF16), "b_gate": b_gate[l][None],
            "conv_w": conv_w[l], "conv_b": conv_b[l][None],
            "q_gain": jnp.tile(q_norm[l], N_HEADS)[None], "k_gain": jnp.tile(k_norm[l], N_KV_HEADS)[None],
            "mem_norm": mem_norm[l][None], "w_mkv": w_mem_kv[l].astype(BF16),
            "p_conv": p_conv[l].astype(BF16), "p_attn": p_attn[l].astype(BF16), "p_mem": p_mem[l].astype(BF16),
            "w_out": w_out[l].astype(BF16),
        }
        hsum = _head_sum_matrix(GROUP_W)
        y_prompt = _layer(y_prompt, mem_prompt, w, _rope_tables(y_prompt.shape[1]) + (hsum,))
        y_sample = _layer(y_sample, mem_sample, w, _rope_tables(y_sample.shape[1]) + (hsum,))
    return (y_prompt, y_sample)
```

```python
import functools

import jax
import jax.numpy as jnp
import numpy as np
from jax import lax
from jax.experimental import pallas as pl
from jax.experimental.pallas import tpu as pltpu

D_MODEL = 1024
D_CONV = 512
N_HEADS = 16
N_KV_HEADS = 4
GROUP = N_HEADS // N_KV_HEADS
HEAD_DIM = 64
AXIS_DIM = HEAD_DIM // 2
ROPE_THETA = 10000.0
GRID_W = 64
N_MEM = 256
N_MEM_HEADS = 4
MEM_HEAD_DIM = 128
N_BRANCH = 3
D_FF = 2816
EPS = 1e-6

LANES = 128
MXU_N = 256
GROUP_W = GROUP * HEAD_DIM
D_MEM = N_MEM_HEADS * MEM_HEAD_DIM
W_IN_Q = 3 * D_CONV
W_IN_K = W_IN_Q + N_HEADS * HEAD_DIM
W_IN_QM = W_IN_K + 2 * N_KV_HEADS * HEAD_DIM
W_IN_G = W_IN_QM + D_MEM

FFN_TM = 1024
MERGE_TM = 512
FFN_CK = MXU_N
ROW_CHUNK = 512
ATTN_TQ = 512
V_ROWS = HEAD_DIM + 16
LOG2_E = 1.4426950408889634

F32 = jnp.float32
BF16 = jnp.bfloat16

_VMEM_LIMIT = 56 * 1024 * 1024


def _resident(shape):
    zeros = (0,) * len(shape)
    return pl.BlockSpec(shape, lambda *_: zeros, pipeline_mode=pl.Buffered(1))


def _rms(x, g):
    ms = jnp.mean(x * x, axis=-1, keepdims=True)
    return x * lax.rsqrt(ms + EPS) * g


def _ffn_kernel(x_ref, pre_ref, w1_ref, w3_ref, w2_ref, post_ref, mixg_ref, o_ref, *rest, emit_u):
    if emit_u:
        u_ref, h_scr = rest
    else:
        (h_scr,) = rest
    x = x_ref[...]
    xn = _rms(x, pre_ref[...]).astype(BF16)
    for c in range(D_FF // FFN_CK):
        cols = slice(c * FFN_CK, (c + 1) * FFN_CK)
        a = jnp.dot(xn, w1_ref[:, cols], preferred_element_type=F32)
        b = jnp.dot(xn, w3_ref[:, cols], preferred_element_type=F32)
        h = a * (1.0 / (1.0 + jnp.exp(-a))) * b
        h_scr[:, cols] = h.astype(BF16)
    y = jnp.dot(h_scr[...], w2_ref[...], preferred_element_type=F32)
    xo = x + _rms(y, post_ref[...])
    o_ref[...] = xo
    if emit_u:
        u_ref[...] = _rms(xo, mixg_ref[...]).astype(BF16)


def _ffn(x2d, pre, w1, w3, w2, post, mixg, emit_u):
    n = x2d.shape[0]
    tm = FFN_TM
    row = lambda i: (i, 0)
    out_shape = [jax.ShapeDtypeStruct((n, D_MODEL), F32)]
    out_specs = [pl.BlockSpec((tm, D_MODEL), row)]
    if emit_u:
        out_shape.append(jax.ShapeDtypeStruct((n, D_MODEL), BF16))
        out_specs.append(pl.BlockSpec((tm, D_MODEL), row))
    res = pl.pallas_call(
        functools.partial(_ffn_kernel, emit_u=emit_u),
        grid=(n // tm,),
        in_specs=[
            pl.BlockSpec((tm, D_MODEL), row),
            _resident((1, D_MODEL)),
            _resident(w1.shape),
            _resident(w3.shape),
            _resident(w2.shape),
            _resident((1, D_MODEL)),
            _resident((1, D_MODEL)),
        ],
        out_specs=out_specs,
        out_shape=out_shape,
        scratch_shapes=[pltpu.VMEM((tm, D_FF), BF16)],
        compiler_params=pltpu.CompilerParams(
            dimension_semantics=("arbitrary",), vmem_limit_bytes=_VMEM_LIMIT),
        name="ffn_u" if emit_u else "ffn",
    )(x2d, pre, w1, w3, w2, post, mixg)
    return res


def _pair_swap(x):
    lane = lax.broadcasted_iota(jnp.int32, x.shape, 1)
    nxt = pltpu.roll(x, LANES - 1, 1)
    prv = pltpu.roll(x, 1, 1)
    return jnp.where((lane & 1) == 0, nxt, prv)


def _norm_rope(r, ssum_ref, g_ref, cos, sin, scale):
    r2 = (r * r).astype(BF16)
    ss = jnp.concatenate(
        [jnp.dot(r2[:, j:j + GROUP_W], ssum_ref[...], preferred_element_type=F32)
         for j in range(0, r.shape[1], GROUP_W)], axis=1)
    rinv = lax.rsqrt(ss * (1.0 / HEAD_DIM) + EPS) * scale
    xg = r * g_ref[...]
    outs = []
    for j in range(r.shape[1] // LANES):
        xs = xg[:, j * LANES:(j + 1) * LANES]
        outs.append(xs * cos + _pair_swap(xs) * sin)
    return jnp.concatenate(outs, axis=1) * rinv


def _group_tile(x, g):
    pair = x[:, (g // 2) * LANES:(g // 2 + 1) * LANES]
    rot = pltpu.roll(pair, HEAD_DIM, 1)
    lane = lax.broadcasted_iota(jnp.int32, pair.shape, 1)
    lo = lane < HEAD_DIM
    t = jnp.where(lo, pair, rot) if g % 2 == 0 else jnp.where(lo, rot, pair)
    return jnp.concatenate([t, t], axis=1)


def _inproj_kernel(u_ref, mem_ref, win_ref, wmkv_ref,
                   convw_ref, convb_ref, qg_ref, kg_ref, memg_ref, cos_ref, sin_ref, hsum_ref,
                   cbz_ref, q_ref, kt_ref, vt_ref, qm_ref, km_ref, vm_ref,
                   z_scr, cb_scr):
    wc_ref = win_ref.at[:, W_IN_Q - 3 * D_CONV:W_IN_Q]
    wq_ref = win_ref.at[:, W_IN_Q:W_IN_K]
    wkv_ref = win_ref.at[:, W_IN_K:W_IN_QM]
    wqm_ref = win_ref.at[:, W_IN_QM:W_IN_G]
    t_len = cbz_ref.shape[0]
    n_chunks = t_len // ROW_CHUNK
    j = pl.program_id(1)

    @pl.when(j == 0)
    def _():
        mn = _rms(mem_ref[...], memg_ref[...]).astype(BF16)
        kv = jnp.dot(mn, wmkv_ref[...], preferred_element_type=F32)
        km_ref[...] = kv[:, :D_MEM].astype(BF16)
        vm_ref[...] = kv[:, D_MEM:].astype(BF16)
        zero_rows = jnp.zeros((8, D_CONV), F32)
        z_scr[0:8, :] = zero_rows
        z_scr[t_len + 8:t_len + 16, :] = zero_rows

    r0 = pl.multiple_of(j * ROW_CHUNK, ROW_CHUNK)
    uc = u_ref[...]

    rc = jnp.dot(uc, wc_ref[...], preferred_element_type=F32)
    z_scr[pl.ds(r0 + 8, ROW_CHUNK), :] = rc[:, 2 * D_CONV:] * rc[:, :D_CONV]
    cb_scr[pl.ds(r0, ROW_CHUNK), :] = rc[:, D_CONV:2 * D_CONV].astype(BF16)

    cos = cos_ref[pl.ds(r0, ROW_CHUNK), :]
    sin = sin_ref[pl.ds(r0, ROW_CHUNK), :]
    rq = jnp.dot(uc, wq_ref[...], preferred_element_type=F32)
    q_ref[...] = _norm_rope(rq, hsum_ref, qg_ref, cos, sin, HEAD_DIM ** -0.5 * LOG2_E).astype(BF16)

    rkv = jnp.dot(uc, wkv_ref[...], preferred_element_type=F32)
    kk = _norm_rope(rkv[:, :GROUP_W], hsum_ref, kg_ref, cos, sin, 1.0)
    vvt = rkv[:, GROUP_W:].T
    tail = (lax.broadcasted_iota(jnp.int32, (V_ROWS - HEAD_DIM, ROW_CHUNK), 0) == 0).astype(BF16)
    for g in range(N_KV_HEADS):
        kt_ref[g] = _group_tile(kk, g).astype(BF16)
        vt_ref[g, 0:HEAD_DIM, :] = vvt[g * HEAD_DIM:(g + 1) * HEAD_DIM, :].astype(BF16)
        vt_ref[g, HEAD_DIM:V_ROWS, :] = tail

    rqm = jnp.dot(uc, wqm_ref[...], preferred_element_type=F32)
    qm_ref[...] = (rqm * (MEM_HEAD_DIM ** -0.5)).astype(BF16)

    @pl.when(j == n_chunks - 1)
    def _():
        cw = convw_ref[...]
        for c in range(n_chunks):
            c0 = c * ROW_CHUNK
            zc = (z_scr[c0 + 7:c0 + 7 + ROW_CHUNK, :] * cw[0:1, :]
                  + z_scr[c0 + 8:c0 + 8 + ROW_CHUNK, :] * cw[1:2, :]
                  + z_scr[c0 + 9:c0 + 9 + ROW_CHUNK, :] * cw[2:3, :]
                  + convb_ref[...])
            cbz_ref[c0:c0 + ROW_CHUNK, :] = (cb_scr[c0:c0 + ROW_CHUNK, :].astype(F32) * zc).astype(BF16)


def _inproj(u, mem, w, tables):
    b, t, _ = u.shape
    cos, sin, hsum = tables
    rows = lambda width: pl.BlockSpec((None, ROW_CHUNK, width), lambda i, j: (i, j, 0))
    tiled = pl.BlockSpec((None, N_KV_HEADS, ROW_CHUNK, GROUP_W), lambda i, j: (i, 0, j, 0))
    vspec = pl.BlockSpec((None, N_KV_HEADS, V_ROWS, ROW_CHUNK), lambda i, j: (i, 0, 0, j))
    memspec = pl.BlockSpec((None, N_MEM, D_MEM), lambda i, j: (i, 0, 0))
    return pl.pallas_call(
        _inproj_kernel,
        grid=(b, t // ROW_CHUNK),
        in_specs=[
            rows(D_MODEL),
            pl.BlockSpec((None, N_MEM, D_MODEL), lambda i, j: (i, 0, 0)),
            _resident(w["w_in"].shape), _resident(w["w_mkv"].shape),
            _resident((3, D_CONV)), _resident((1, D_CONV)),
            _resident((1, N_HEADS * HEAD_DIM)), _resident((1, GROUP_W)), _resident((1, D_MODEL)),
            _resident(cos.shape), _resident(sin.shape), _resident(hsum.shape),
        ],
        out_specs=[pl.BlockSpec((None, t, D_CONV), lambda i, j: (i, 0, 0)), rows(N_HEADS * HEAD_DIM),
                   tiled, vspec, rows(D_MEM), memspec, memspec],
        out_shape=[
            jax.ShapeDtypeStruct((b, t, D_CONV), BF16),
            jax.ShapeDtypeStruct((b, t, N_HEADS * HEAD_DIM), BF16),
            jax.ShapeDtypeStruct((b, N_KV_HEADS, t, GROUP_W), BF16),
            jax.ShapeDtypeStruct((b, N_KV_HEADS, V_ROWS, t), BF16),
            jax.ShapeDtypeStruct((b, t, D_MEM), BF16),
            jax.ShapeDtypeStruct((b, N_MEM, D_MEM), BF16),
            jax.ShapeDtypeStruct((b, N_MEM, D_MEM), BF16),
        ],
        scratch_shapes=[pltpu.VMEM((t + 16, D_CONV), F32), pltpu.VMEM((t, D_CONV), BF16)],
        compiler_params=pltpu.CompilerParams(
            dimension_semantics=("arbitrary", "arbitrary"), vmem_limit_bytes=_VMEM_LIMIT),
        name="inproj",
    )(u, mem, w["w_in"], w["w_mkv"],
      w["conv_w"], w["conv_b"], w["q_gain"], w["k_gain"], w["mem_norm"], cos, sin, hsum)


def _attn_step(q, kt_ref, vt_ref, s_new, m_new, s_cur, m_cur):
    tq = q.shape[0]
    t = s_new.shape[0]
    cols = GROUP * tq
    qf = q.astype(F32)
    head = lax.broadcasted_iota(jnp.int32, qf.shape, 1) // HEAD_DIM
    qs = jnp.concatenate([jnp.where(head == h, qf, 0.0) for h in range(GROUP)], axis=0).astype(BF16)
    st = lax.dot_general(kt_ref[...], qs, (((1,), (1,)), ((), ())), preferred_element_type=F32)
    s_new[:, 0:cols] = st
    m_new[...] = jnp.max(st, axis=0, keepdims=True)

    m = m_cur[...]
    accs = []
    for c in range(cols // MXU_N):
        csl = slice(c * MXU_N, (c + 1) * MXU_N)
        acc = None
        for k in range(t // MXU_N):
            ksl = slice(k * MXU_N, (k + 1) * MXU_N)
            p = jnp.exp2(s_cur[ksl, csl] - m[:, csl]).astype(BF16)
            d = jnp.dot(vt_ref[:, ksl], p, preferred_element_type=F32)
            acc = d if acc is None else acc + d
        accs.append(acc)
    ot = jnp.concatenate(accs, axis=1)
    on = ot[0:HEAD_DIM, :] * (1.0 / ot[HEAD_DIM:HEAD_DIM + 1, :])
    out_t = jnp.concatenate([on[:, h * tq:(h + 1) * tq] for h in range(GROUP)], axis=0)
    return out_t.T.astype(BF16)


def _attn_kernel(q_ref, kt_ref, vt_ref, o_ref, s0, s1, m0, m1):
    n = pl.program_id(0)

    @pl.when(n == 0)
    def _():
        s1[...] = jnp.zeros_like(s1)
        m1[...] = jnp.zeros_like(m1)

    @pl.when(n % 2 == 0)
    def _():
        o_ref[...] = _attn_step(q_ref[...], kt_ref, vt_ref, s0, m0, s1, m1)

    @pl.when(n % 2 == 1)
    def _():
        o_ref[...] = _attn_step(q_ref[...], kt_ref, vt_ref, s1, m1, s0, m0)


def _attn(q, kt, vt):
    b, t, _ = q.shape
    tq = ATTN_TQ
    tiles_per_seq = t // tq
    n_tiles = b * N_KV_HEADS * tiles_per_seq

    def place(i):
        return i // (N_KV_HEADS * tiles_per_seq), (i // tiles_per_seq) % N_KV_HEADS, i % tiles_per_seq

    def score_tile(n):
        return place(jnp.minimum(n, n_tiles - 1))

    def value_tile(n):
        return place(jnp.maximum(n - 1, 0))

    def q_map(n):
        bi, g, j = score_tile(n)
        return bi, j, g

    def kt_map(n):
        bi, g, _ = score_tile(n)
        return bi, g, 0, 0

    def vt_map(n):
        bi, g, _ = value_tile(n)
        return bi, g, 0, 0

    def o_map(n):
        bi, g, j = value_tile(n)
        return bi, j, g

    cols = GROUP * tq
    return pl.pallas_call(
        _attn_kernel,
        grid=(n_tiles + 1,),
        in_specs=[pl.BlockSpec((None, tq, GROUP_W), q_map),
                  pl.BlockSpec((None, None, t, GROUP_W), kt_map),
                  pl.BlockSpec((None, None, V_ROWS, t), vt_map)],
        out_specs=pl.BlockSpec((None, tq, GROUP_W), o_map),
        out_shape=jax.ShapeDtypeStruct((b, t, N_HEADS * HEAD_DIM), BF16),
        scratch_shapes=[pltpu.VMEM((t, cols + LANES), F32), pltpu.VMEM((t, cols + LANES), F32),
                        pltpu.VMEM((1, cols), F32), pltpu.VMEM((1, cols), F32)],
        compiler_params=pltpu.CompilerParams(
            dimension_semantics=("arbitrary",), vmem_limit_bytes=_VMEM_LIMIT),
        name="attn",
    )(q, kt, vt)


def _merge_kernel(x_ref, cbz_ref, attn_ref, qm_ref, km_ref, vm_ref,
                  pconv_ref, pattn_ref, pmem_ref, win_ref, bg_ref, wout_ref, mixg_ref, postg_ref, o_ref):
    x = x_ref[...]
    u = _rms(x, mixg_ref[...]).astype(BF16)

    qm = qm_ref[...]
    heads = []
    for h in range(N_MEM_HEADS):
        cols = slice(h * MEM_HEAD_DIM, (h + 1) * MEM_HEAD_DIM)
        s = lax.dot_general(qm[:, cols], km_ref[:, cols], (((1,), (1,)), ((), ())),
                            preferred_element_type=F32)
        p = jnp.exp(s - jnp.max(s, axis=-1, keepdims=True))
        l = jnp.sum(p, axis=-1, keepdims=True)
        oh = jnp.dot(p.astype(BF16), vm_ref[:, cols], preferred_element_type=F32)
        heads.append(oh * (1.0 / l))
    memo = jnp.concatenate(heads, axis=1).astype(BF16)

    branches = (
        (cbz_ref[...], pconv_ref),
        (attn_ref[...], pattn_ref),
        (memo, pmem_ref),
    )
    merged = None
    for br, (act, proj_ref) in enumerate(branches):
        cols = slice(br * D_MODEL, (br + 1) * D_MODEL)
        gate = jnp.dot(u, win_ref[:, W_IN_G + br * D_MODEL:W_IN_G + (br + 1) * D_MODEL],
                       preferred_element_type=F32) + bg_ref[:, cols]
        gate = 1.0 / (1.0 + jnp.exp(-gate))
        term = gate * jnp.dot(act, proj_ref[...], preferred_element_type=F32)
        merged = term if merged is None else merged + term
    y = jnp.dot(merged.astype(BF16), wout_ref[...], preferred_element_type=F32)
    o_ref[...] = x + _rms(y, postg_ref[...])


def _merge(x, cbz, attn, qm, km, vm, w):
    b, t, _ = x.shape
    tm = MERGE_TM
    tok = lambda width: pl.BlockSpec((None, tm, width), lambda i, j: (i, j, 0))
    memspec = pl.BlockSpec((None, N_MEM, D_MEM), lambda i, j: (i, 0, 0))
    return pl.pallas_call(
        _merge_kernel,
        grid=(b, t // tm),
        in_specs=[
            tok(D_MODEL), tok(D_CONV), tok(N_HEADS * HEAD_DIM), tok(D_MEM), memspec, memspec,
            _resident(w["p_conv"].shape), _resident(w["p_attn"].shape), _resident(w["p_mem"].shape),
            _resident(w["w_in"].shape), _resident((1, N_BRANCH * D_MODEL)), _resident(w["w_out"].shape),
            _resident((1, D_MODEL)), _resident((1, D_MODEL)),
        ],
        out_specs=tok(D_MODEL),
        out_shape=jax.ShapeDtypeStruct((b, t, D_MODEL), F32),
        compiler_params=pltpu.CompilerParams(
            dimension_semantics=("arbitrary", "arbitrary"), vmem_limit_bytes=_VMEM_LIMIT),
        name="merge",
    )(x, cbz, attn, qm, km, vm, w["p_conv"], w["p_attn"], w["p_mem"], w["w_in"], w["b_gate"],
      w["w_out"], w["mix_pre"], w["mix_post"])


def _rope_tables(t):
    pos = np.arange(t)
    row = (pos // GRID_W).astype(np.float64)
    col = (pos % GRID_W).astype(np.float64)
    inv = 1.0 / (ROPE_THETA ** (np.arange(0, AXIS_DIM, 2, dtype=np.float64) / AXIS_DIM))
    ang = np.concatenate([row[:, None] * inv, col[:, None] * inv], axis=-1)
    cos = np.repeat(np.cos(ang), 2, axis=-1)
    sin = np.stack([-np.sin(ang), np.sin(ang)], axis=-1).reshape(t, HEAD_DIM)
    reps = LANES // HEAD_DIM
    return (jnp.asarray(np.tile(cos, (1, reps)), dtype=F32), jnp.asarray(np.tile(sin, (1, reps)), dtype=F32))


def _head_sum_matrix(width):
    head = np.arange(width) // HEAD_DIM
    return jnp.asarray(head[:, None] == head[None, :], dtype=BF16)


def _layer(x, mem, w, tables):
    b, t, d = x.shape
    x1, u = _ffn(x.reshape(b * t, d), w["ffn1_pre"], w["ffn1_w1"], w["ffn1_w3"], w["ffn1_w2"], w["ffn1_post"],
                 w["mix_pre"], emit_u=True)
    cbz, q, kt, vt, qm, km, vm = _inproj(u.reshape(b, t, d), mem, w, tables)
    attn = _attn(q, kt, vt)
    x2 = _merge(x1.reshape(b, t, d), cbz, attn, qm, km, vm, w)
    (y,) = _ffn(x2.reshape(b * t, d), w["ffn2_pre"], w["ffn2_w1"], w["ffn2_w3"], w["ffn2_w2"], w["ffn2_post"],
                w["mix_pre"], emit_u=False)
    return y.reshape(b, t, d)


def kernel(x_prompt, x_sample, mem_prompt, mem_sample, ffn1_pre, ffn1_w1, ffn1_w3, ffn1_w2, ffn1_post, mix_pre, w_in, conv_w, conv_b, p_conv, q_norm, k_norm, p_attn, mem_norm, w_mem_kv, p_mem, b_gate, w_out, mix_post, ffn2_pre, ffn2_w1, ffn2_w3, ffn2_w2, ffn2_post):
    y_prompt, y_sample = x_prompt, x_sample
    for l in range(ffn1_pre.shape[0]):
        w = {
            "ffn1_pre": ffn1_pre[l][None], "ffn1_post": 0.5 * ffn1_post[l][None],
            "ffn1_w1": ffn1_w1[l].astype(BF16), "ffn1_w3": ffn1_w3[l].astype(BF16),
            "ffn1_w2": ffn1_w2[l].astype(BF16),
            "ffn2_pre": ffn2_pre[l][None], "ffn2_post": 0.5 * ffn2_post[l][None],
            "ffn2_w1": ffn2_w1[l].astype(BF16), "ffn2_w3": ffn2_w3[l].astype(BF16),
            "ffn2_w2": ffn2_w2[l].astype(BF16),
            "mix_pre": mix_pre[l][None], "mix_post": mix_post[l][None],
            "w_in": w_in[l].astype(BF16), "b_gate": b_gate[l][None],
            "conv_w": conv_w[l], "conv_b": conv_b[l][None],
            "q_gain": jnp.tile(q_norm[l], N_HEADS)[None], "k_gain": jnp.tile(k_norm[l], N_KV_HEADS)[None],
            "mem_norm": mem_norm[l][None], "w_mkv": w_mem_kv[l].astype(BF16),
            "p_conv": p_conv[l].astype(BF16), "p_attn": p_attn[l].astype(BF16), "p_mem": p_mem[l].astype(BF16),
            "w_out": w_out[l].astype(BF16),
        }
        hsum = _head_sum_matrix(GROUP_W)
        y_prompt = _layer(y_prompt, mem_prompt, w, _rope_tables(y_prompt.shape[1]) + (hsum,))
        y_sample = _layer(y_sample, mem_sample, w, _rope_tables(y_sample.shape[1]) + (hsum,))
    return (y_prompt, y_sample)
```

```python
import functools

import jax
import jax.numpy as jnp
import numpy as np
from jax import lax
from jax.experimental import pallas as pl
from jax.experimental.pallas import tpu as pltpu

D_MODEL = 1024
D_CONV = 512
N_HEADS = 16
N_KV_HEADS = 4
GROUP = N_HEADS // N_KV_HEADS
HEAD_DIM = 64
AXIS_DIM = HEAD_DIM // 2
ROPE_THETA = 10000.0
GRID_W = 64
N_MEM = 256
N_MEM_HEADS = 4
MEM_HEAD_DIM = 128
N_BRANCH = 3
D_FF = 2816
EPS = 1e-6

LANES = 128
MXU_N = 256
GROUP_W = GROUP * HEAD_DIM
D_MEM = N_MEM_HEADS * MEM_HEAD_DIM
W_IN_Q = 3 * D_CONV
W_IN_K = W_IN_Q + N_HEADS * HEAD_DIM
W_IN_QM = W_IN_K + 2 * N_KV_HEADS * HEAD_DIM
W_IN_G = W_IN_QM + D_MEM

FFN_TM = 1024
MERGE_TM = 512
FFN_CK = MXU_N
ROW_CHUNK = 512
ATTN_TQ = 512
V_ROWS = MXU_N
LOG2_E = 1.4426950408889634

F32 = jnp.float32
BF16 = jnp.bfloat16

_VMEM_LIMIT = 56 * 1024 * 1024


def _resident(shape):
    zeros = (0,) * len(shape)
    return pl.BlockSpec(shape, lambda *_: zeros, pipeline_mode=pl.Buffered(1))


def _rms(x, g):
    ms = jnp.mean(x * x, axis=-1, keepdims=True)
    return x * lax.rsqrt(ms + EPS) * g


def _ffn_kernel(x_ref, pre_ref, w1_ref, w3_ref, w2_ref, post_ref, mixg_ref, o_ref, *rest, emit_u):
    if emit_u:
        u_ref, h_scr = rest
    else:
        (h_scr,) = rest
    x = x_ref[...]
    xn = _rms(x, pre_ref[...]).astype(BF16)
    for c in range(D_FF // FFN_CK):
        cols = slice(c * FFN_CK, (c + 1) * FFN_CK)
        a = jnp.dot(xn, w1_ref[:, cols], preferred_element_type=F32)
        b = jnp.dot(xn, w3_ref[:, cols], preferred_element_type=F32)
        h = a * (1.0 / (1.0 + jnp.exp(-a))) * b
        h_scr[:, cols] = h.astype(BF16)
    y = jnp.dot(h_scr[...], w2_ref[...], preferred_element_type=F32)
    xo = x + _rms(y, post_ref[...])
    o_ref[...] = xo
    if emit_u:
        u_ref[...] = _rms(xo, mixg_ref[...]).astype(BF16)


def _ffn(x2d, pre, w1, w3, w2, post, mixg, emit_u):
    n = x2d.shape[0]
    tm = FFN_TM
    row = lambda i: (i, 0)
    out_shape = [jax.ShapeDtypeStruct((n, D_MODEL), F32)]
    out_specs = [pl.BlockSpec((tm, D_MODEL), row)]
    if emit_u:
        out_shape.append(jax.ShapeDtypeStruct((n, D_MODEL), BF16))
        out_specs.append(pl.BlockSpec((tm, D_MODEL), row))
    res = pl.pallas_call(
        functools.partial(_ffn_kernel, emit_u=emit_u),
        grid=(n // tm,),
        in_specs=[
            pl.BlockSpec((tm, D_MODEL), row),
            _resident((1, D_MODEL)),
            _resident(w1.shape),
            _resident(w3.shape),
            _resident(w2.shape),
            _resident((1, D_MODEL)),
            _resident((1, D_MODEL)),
        ],
        out_specs=out_specs,
        out_shape=out_shape,
        scratch_shapes=[pltpu.VMEM((tm, D_FF), BF16)],
        compiler_params=pltpu.CompilerParams(
            dimension_semantics=("arbitrary",), vmem_limit_bytes=_VMEM_LIMIT),
        name="ffn_u" if emit_u else "ffn",
    )(x2d, pre, w1, w3, w2, post, mixg)
    return res


def _pair_swap(x):
    lane = lax.broadcasted_iota(jnp.int32, x.shape, 1)
    nxt = pltpu.roll(x, LANES - 1, 1)
    prv = pltpu.roll(x, 1, 1)
    return jnp.where((lane & 1) == 0, nxt, prv)


def _norm_rope(r, ssum_ref, g_ref, cos, sin, scale):
    r2 = (r * r).astype(BF16)
    ss = jnp.concatenate(
        [jnp.dot(r2[:, j:j + GROUP_W], ssum_ref[...], preferred_element_type=F32)
         for j in range(0, r.shape[1], GROUP_W)], axis=1)
    rinv = lax.rsqrt(ss * (1.0 / HEAD_DIM) + EPS) * scale
    xg = r * g_ref[...]
    outs = []
    for j in range(r.shape[1] // LANES):
        xs = xg[:, j * LANES:(j + 1) * LANES]
        outs.append(xs * cos + _pair_swap(xs) * sin)
    return jnp.concatenate(outs, axis=1) * rinv


def _group_tile(x, g):
    pair = x[:, (g // 2) * LANES:(g // 2 + 1) * LANES]
    rot = pltpu.roll(pair, HEAD_DIM, 1)
    lane = lax.broadcasted_iota(jnp.int32, pair.shape, 1)
    lo = lane < HEAD_DIM
    t = jnp.where(lo, pair, rot) if g % 2 == 0 else jnp.where(lo, rot, pair)
    return jnp.concatenate([t, t], axis=1)


def _inproj_kernel(u_ref, mem_ref, win_ref, wmkv_ref,
                   convw_ref, convb_ref, qg_ref, kg_ref, memg_ref, cos_ref, sin_ref, hsum_ref,
                   cbz_ref, q_ref, kt_ref, vt_ref, qm_ref, km_ref, vm_ref,
                   z_scr, cb_scr):
    wc_ref = win_ref.at[:, W_IN_Q - 3 * D_CONV:W_IN_Q]
    wq_ref = win_ref.at[:, W_IN_Q:W_IN_K]
    wkv_ref = win_ref.at[:, W_IN_K:W_IN_QM]
    wqm_ref = win_ref.at[:, W_IN_QM:W_IN_G]
    t_len = cbz_ref.shape[0]
    n_chunks = t_len // ROW_CHUNK
    j = pl.program_id(1)

    @pl.when(j == 0)
    def _():
        mn = _rms(mem_ref[...], memg_ref[...]).astype(BF16)
        kv = jnp.dot(mn, wmkv_ref[...], preferred_element_type=F32)
        km_ref[...] = kv[:, :D_MEM].astype(BF16)
        vm_ref[...] = kv[:, D_MEM:].astype(BF16)
        zero_rows = jnp.zeros((8, D_CONV), F32)
        z_scr[0:8, :] = zero_rows
        z_scr[t_len + 8:t_len + 16, :] = zero_rows

    r0 = pl.multiple_of(j * ROW_CHUNK, ROW_CHUNK)
    uc = u_ref[...]

    rc = jnp.dot(uc, wc_ref[...], preferred_element_type=F32)
    z_scr[pl.ds(r0 + 8, ROW_CHUNK), :] = rc[:, 2 * D_CONV:] * rc[:, :D_CONV]
    cb_scr[pl.ds(r0, ROW_CHUNK), :] = rc[:, D_CONV:2 * D_CONV].astype(BF16)

    cos = cos_ref[pl.ds(r0, ROW_CHUNK), :]
    sin = sin_ref[pl.ds(r0, ROW_CHUNK), :]
    rq = jnp.dot(uc, wq_ref[...], preferred_element_type=F32)
    q_ref[...] = _norm_rope(rq, hsum_ref, qg_ref, cos, sin, HEAD_DIM ** -0.5 * LOG2_E).astype(BF16)

    rkv = jnp.dot(uc, wkv_ref[...], preferred_element_type=F32)
    kk = _norm_rope(rkv[:, :GROUP_W], hsum_ref, kg_ref, cos, sin, 1.0)
    vvt = rkv[:, GROUP_W:].T
    tail = (lax.broadcasted_iota(jnp.int32, (V_ROWS - HEAD_DIM, ROW_CHUNK), 0) == 0).astype(BF16)
    for g in range(N_KV_HEADS):
        kt_ref[g] = _group_tile(kk, g).astype(BF16)
        vt_ref[g, 0:HEAD_DIM, :] = vvt[g * HEAD_DIM:(g + 1) * HEAD_DIM, :].astype(BF16)
        vt_ref[g, HEAD_DIM:V_ROWS, :] = tail

    rqm = jnp.dot(uc, wqm_ref[...], preferred_element_type=F32)
    qm_ref[...] = (rqm * (MEM_HEAD_DIM ** -0.5)).astype(BF16)

    @pl.when(j == n_chunks - 1)
    def _():
        cw = convw_ref[...]
        for c in range(n_chunks):
            c0 = c * ROW_CHUNK
            zc = (z_scr[c0 + 7:c0 + 7 + ROW_CHUNK, :] * cw[0:1, :]
                  + z_scr[c0 + 8:c0 + 8 + ROW_CHUNK, :] * cw[1:2, :]
                  + z_scr[c0 + 9:c0 + 9 + ROW_CHUNK, :] * cw[2:3, :]
                  + convb_ref[...])
            cbz_ref[c0:c0 + ROW_CHUNK, :] = (cb_scr[c0:c0 + ROW_CHUNK, :].astype(F32) * zc).astype(BF16)


def _inproj(u, mem, w, tables):
    b, t, _ = u.shape
    cos, sin, hsum = tables
    rows = lambda width: pl.BlockSpec((None, ROW_CHUNK, width), lambda i, j: (i, j, 0))
    tiled = pl.BlockSpec((None, N_KV_HEADS, ROW_CHUNK, GROUP_W), lambda i, j: (i, 0, j, 0))
    vspec = pl.BlockSpec((None, N_KV_HEADS, V_ROWS, ROW_CHUNK), lambda i, j: (i, 0, 0, j))
    memspec = pl.BlockSpec((None, N_MEM, D_MEM), lambda i, j: (i, 0, 0))
    return pl.pallas_call(
        _inproj_kernel,
        grid=(b, t // ROW_CHUNK),
        in_specs=[
            rows(D_MODEL),
            pl.BlockSpec((None, N_MEM, D_MODEL), lambda i, j: (i, 0, 0)),
            _resident(w["w_in"].shape), _resident(w["w_mkv"].shape),
            _resident((3, D_CONV)), _resident((1, D_CONV)),
            _resident((1, N_HEADS * HEAD_DIM)), _resident((1, GROUP_W)), _resident((1, D_MODEL)),
            _resident(cos.shape), _resident(sin.shape), _resident(hsum.shape),
        ],
        out_specs=[pl.BlockSpec((None, t, D_CONV), lambda i, j: (i, 0, 0)), rows(N_HEADS * HEAD_DIM),
                   tiled, vspec, rows(D_MEM), memspec, memspec],
        out_shape=[
            jax.ShapeDtypeStruct((b, t, D_CONV), BF16),
            jax.ShapeDtypeStruct((b, t, N_HEADS * HEAD_DIM), BF16),
            jax.ShapeDtypeStruct((b, N_KV_HEADS, t, GROUP_W), BF16),
            jax.ShapeDtypeStruct((b, N_KV_HEADS, V_ROWS, t), BF16),
            jax.ShapeDtypeStruct((b, t, D_MEM), BF16),
            jax.ShapeDtypeStruct((b, N_MEM, D_MEM), BF16),
            jax.ShapeDtypeStruct((b, N_MEM, D_MEM), BF16),
        ],
        scratch_shapes=[pltpu.VMEM((t + 16, D_CONV), F32), pltpu.VMEM((t, D_CONV), BF16)],
        compiler_params=pltpu.CompilerParams(
            dimension_semantics=("arbitrary", "arbitrary"), vmem_limit_bytes=_VMEM_LIMIT),
        name="inproj",
    )(u, mem, w["w_in"], w["w_mkv"],
      w["conv_w"], w["conv_b"], w["q_gain"], w["k_gain"], w["mem_norm"], cos, sin, hsum)


def _attn_step(q, kt_ref, vt_ref, s_new, m_new, s_cur, m_cur):
    tq = q.shape[0]
    t = s_new.shape[0]
    cols = GROUP * tq
    qf = q.astype(F32)
    head = lax.broadcasted_iota(jnp.int32, qf.shape, 1) // HEAD_DIM
    qs = jnp.concatenate([jnp.where(head == h, qf, 0.0) for h in range(GROUP)], axis=0).astype(BF16)
    st = lax.dot_general(kt_ref[...], qs, (((1,), (1,)), ((), ())), preferred_element_type=F32)
    s_new[:, 0:cols] = st
    m_new[...] = jnp.max(st, axis=0, keepdims=True)

    m = m_cur[...]
    accs = []
    for c in range(cols // MXU_N):
        csl = slice(c * MXU_N, (c + 1) * MXU_N)
        acc = None
        for k in range(t // MXU_N):
            ksl = slice(k * MXU_N, (k + 1) * MXU_N)
            p = jnp.exp2(s_cur[ksl, csl] - m[:, csl]).astype(BF16)
            d = jnp.dot(vt_ref[:, ksl], p, preferred_element_type=F32)
            acc = d if acc is None else acc + d
        accs.append(acc)
    ot = jnp.concatenate(accs, axis=1)
    on = ot[0:HEAD_DIM, :] * (1.0 / ot[HEAD_DIM:HEAD_DIM + 1, :])
    out_t = jnp.concatenate([on[:, h * tq:(h + 1) * tq] for h in range(GROUP)], axis=0)
    return out_t.T.astype(BF16)


def _attn_kernel(q_ref, kt_ref, vt_ref, o_ref, s0, s1, m0, m1):
    n = pl.program_id(0)

    @pl.when(n == 0)
    def _():
        s1[...] = jnp.zeros_like(s1)
        m1[...] = jnp.zeros_like(m1)

    @pl.when(n % 2 == 0)
    def _():
        o_ref[...] = _attn_step(q_ref[...], kt_ref, vt_ref, s0, m0, s1, m1)

    @pl.when(n % 2 == 1)
    def _():
        o_ref[...] = _attn_step(q_ref[...], kt_ref, vt_ref, s1, m1, s0, m0)


def _attn(q, kt, vt):
    b, t, _ = q.shape
    tq = ATTN_TQ
    tiles_per_seq = t // tq
    n_tiles = b * N_KV_HEADS * tiles_per_seq

    def place(i):
        return i // (N_KV_HEADS * tiles_per_seq), (i // tiles_per_seq) % N_KV_HEADS, i % tiles_per_seq

    def score_tile(n):
        return place(jnp.minimum(n, n_tiles - 1))

    def value_tile(n):
        return place(jnp.maximum(n - 1, 0))

    def q_map(n):
        bi, g, j = score_tile(n)
        return bi, j, g

    def kt_map(n):
        bi, g, _ = score_tile(n)
        return bi, g, 0, 0

    def vt_map(n):
        bi, g, _ = value_tile(n)
        return bi, g, 0, 0

    def o_map(n):
        bi, g, j = value_tile(n)
        return bi, j, g

    cols = GROUP * tq
    return pl.pallas_call(
        _attn_kernel,
        grid=(n_tiles + 1,),
        in_specs=[pl.BlockSpec((None, tq, GROUP_W), q_map),
                  pl.BlockSpec((None, None, t, GROUP_W), kt_map),
                  pl.BlockSpec((None, None, V_ROWS, t), vt_map)],
        out_specs=pl.BlockSpec((None, tq, GROUP_W), o_map),
        out_shape=jax.ShapeDtypeStruct((b, t, N_HEADS * HEAD_DIM), BF16),
        scratch_shapes=[pltpu.VMEM((t, cols + LANES), F32), pltpu.VMEM((t, cols + LANES), F32),
                        pltpu.VMEM((1, cols), F32), pltpu.VMEM((1, cols), F32)],
        compiler_params=pltpu.CompilerParams(
            dimension_semantics=("arbitrary",), vmem_limit_bytes=_VMEM_LIMIT),
        name="attn",
    )(q, kt, vt)


def _merge_kernel(x_ref, cbz_ref, attn_ref, qm_ref, km_ref, vm_ref,
                  pconv_ref, pattn_ref, pmem_ref, win_ref, bg_ref, wout_ref, mixg_ref, postg_ref, o_ref):
    x = x_ref[...]
    u = _rms(x, mixg_ref[...]).astype(BF16)

    qm = qm_ref[...]
    heads = []
    for h in range(N_MEM_HEADS):
        cols = slice(h * MEM_HEAD_DIM, (h + 1) * MEM_HEAD_DIM)
        s = lax.dot_general(qm[:, cols], km_ref[:, cols], (((1,), (1,)), ((), ())),
                            preferred_element_type=F32)
        p = jnp.exp(s - jnp.max(s, axis=-1, keepdims=True))
        l = jnp.sum(p, axis=-1, keepdims=True)
        oh = jnp.dot(p.astype(BF16), vm_ref[:, cols], preferred_element_type=F32)
        heads.append(oh * (1.0 / l))
    memo = jnp.concatenate(heads, axis=1).astype(BF16)

    branches = (
        (cbz_ref[...], pconv_ref),
        (attn_ref[...], pattn_ref),
        (memo, pmem_ref),
    )
    merged = None
    for br, (act, proj_ref) in enumerate(branches):
        cols = slice(br * D_MODEL, (br + 1) * D_MODEL)
        gate = jnp.dot(u, win_ref[:, W_IN_G + br * D_MODEL:W_IN_G + (br + 1) * D_MODEL],
                       preferred_element_type=F32) + bg_ref[:, cols]
        gate = 1.0 / (1.0 + jnp.exp(-gate))
        term = gate * jnp.dot(act, proj_ref[...], preferred_element_type=F32)
        merged = term if merged is None else merged + term
    y = jnp.dot(merged.astype(BF16), wout_ref[...], preferred_element_type=F32)
    o_ref[...] = x + _rms(y, postg_ref[...])


def _merge(x, cbz, attn, qm, km, vm, w):
    b, t, _ = x.shape
    tm = MERGE_TM
    tok = lambda width: pl.BlockSpec((None, tm, width), lambda i, j: (i, j, 0))
    memspec = pl.BlockSpec((None, N_MEM, D_MEM), lambda i, j: (i, 0, 0))
    return pl.pallas_call(
        _merge_kernel,
        grid=(b, t // tm),
        in_specs=[
            tok(D_MODEL), tok(D_CONV), tok(N_HEADS * HEAD_DIM), tok(D_MEM), memspec, memspec,
            _resident(w["p_conv"].shape), _resident(w["p_attn"].shape), _resident(w["p_mem"].shape),
            _resident(w["w_in"].shape), _resident((1, N_BRANCH * D_MODEL)), _resident(w["w_out"].shape),
            _resident((1, D_MODEL)), _resident((1, D_MODEL)),
        ],
        out_specs=tok(D_MODEL),
        out_shape=jax.ShapeDtypeStruct((b, t, D_MODEL), F32),
        compiler_params=pltpu.CompilerParams(
            dimension_semantics=("arbitrary", "arbitrary"), vmem_limit_bytes=_VMEM_LIMIT),
        name="merge",
    )(x, cbz, attn, qm, km, vm, w["p_conv"], w["p_attn"], w["p_mem"], w["w_in"], w["b_gate"],
      w["w_out"], w["mix_pre"], w["mix_post"])


def _rope_tables(t):
    pos = np.arange(t)
    row = (pos // GRID_W).astype(np.float64)
    col = (pos % GRID_W).astype(np.float64)
    inv = 1.0 / (ROPE_THETA ** (np.arange(0, AXIS_DIM, 2, dtype=np.float64) / AXIS_DIM))
    ang = np.concatenate([row[:, None] * inv, col[:, None] * inv], axis=-1)
    cos = np.repeat(np.cos(ang), 2, axis=-1)
    sin = np.stack([-np.sin(ang), np.sin(ang)], axis=-1).reshape(t, HEAD_DIM)
    reps = LANES // HEAD_DIM
    return (jnp.asarray(np.tile(cos, (1, reps)), dtype=F32), jnp.asarray(np.tile(sin, (1, reps)), dtype=F32))


def _head_sum_matrix(width):
    head = np.arange(width) // HEAD_DIM
    return jnp.asarray(head[:, None] == head[None, :], dtype=BF16)


def _layer(x, mem, w, tables):
    b, t, d = x.shape
    x1, u = _ffn(x.reshape(b * t, d), w["ffn1_pre"], w["ffn1_w1"], w["ffn1_w3"], w["ffn1_w2"], w["ffn1_post"],
                 w["mix_pre"], emit_u=True)
    cbz, q, kt, vt, qm, km, vm = _inproj(u.reshape(b, t, d), mem, w, tables)
    attn = _attn(q, kt, vt)
    x2 = _merge(x1.reshape(b, t, d), cbz, attn, qm, km, vm, w)
    (y,) = _ffn(x2.reshape(b * t, d), w["ffn2_pre"], w["ffn2_w1"], w["ffn2_w3"], w["ffn2_w2"], w["ffn2_post"],
                w["mix_pre"], emit_u=False)
    return y.reshape(b, t, d)


def kernel(x_prompt, x_sample, mem_prompt, mem_sample, ffn1_pre, ffn1_w1, ffn1_w3, ffn1_w2, ffn1_post, mix_pre, w_in, conv_w, conv_b, p_conv, q_norm, k_norm, p_attn, mem_norm, w_mem_kv, p_mem, b_gate, w_out, mix_post, ffn2_pre, ffn2_w1, ffn2_w3, ffn2_w2, ffn2_post):
    y_prompt, y_sample = x_prompt, x_sample
    for l in range(ffn1_pre.shape[0]):
        w = {
            "ffn1_pre": ffn1_pre[l][None], "ffn1_post": 0.5 * ffn1_post[l][None],
            "ffn1_w1": ffn1_w1[l].astype(BF16), "ffn1_w3": ffn1_w3[l].astype(BF16),
            "ffn1_w2": ffn1_w2[l].astype(BF16),
            "ffn2_pre": ffn2_pre[l][None], "ffn2_post": 0.5 * ffn2_post[l][None],
            "ffn2_w1": ffn2_w1[l].astype(BF16), "ffn2_w3": ffn2_w3[l].astype(BF16),
            "ffn2_w2": ffn2_w2[l].astype(BF16),
            "mix_pre": mix_pre[l][None], "mix_post": mix_post[l][None],
            "w_in": w_in[l].astype(BF16), "b_gate": b_gate[l][None],
            "conv_w": conv_w[l], "conv_b": conv_b[l][None],
            "q_gain": jnp.tile(q_norm[l], N_HEADS)[None], "k_gain": jnp.tile(k_norm[l], N_KV_HEADS)[None],
            "mem_norm": mem_norm[l][None], "w_mkv": w_mem_kv[l].astype(BF16),
            "p_conv": p_conv[l].astype(BF16), "p_attn": p_attn[l].astype(BF16), "p_mem": p_mem[l].astype(BF16),
            "w_out": w_out[l].astype(BF16),
        }
        hsum = _head_sum_matrix(GROUP_W)
        y_prompt = _layer(y_prompt, mem_prompt, w, _rope_tables(y_prompt.shape[1]) + (hsum,))
        y_sample = _layer(y_sample, mem_sample, w, _rope_tables(y_sample.shape[1]) + (hsum,))
    return (y_prompt, y_sample)
```

```python
import functools

import jax
import jax.numpy as jnp
import numpy as np
from jax import lax
from jax.experimental import pallas as pl
from jax.experimental.pallas import tpu as pltpu

D_MODEL = 1024
D_CONV = 512
N_HEADS = 16
N_KV_HEADS = 4
GROUP = N_HEADS // N_KV_HEADS
HEAD_DIM = 64
AXIS_DIM = HEAD_DIM // 2
ROPE_THETA = 10000.0
GRID_W = 64
N_MEM = 256
N_MEM_HEADS = 4
MEM_HEAD_DIM = 128
N_BRANCH = 3
D_FF = 2816
EPS = 1e-6

LANES = 128
MXU_N = 256
GROUP_W = GROUP * HEAD_DIM
D_MEM = N_MEM_HEADS * MEM_HEAD_DIM
W_IN_Q = 3 * D_CONV
W_IN_K = W_IN_Q + N_HEADS * HEAD_DIM
W_IN_QM = W_IN_K + 2 * N_KV_HEADS * HEAD_DIM
W_IN_G = W_IN_QM + D_MEM

FFN_TM = 1024
MERGE_TM = 1024
FFN_CK = MXU_N
ROW_CHUNK = 512
ATTN_TQ = 512
V_ROWS = MXU_N
LOG2_E = 1.4426950408889634

F32 = jnp.float32
BF16 = jnp.bfloat16

_VMEM_LIMIT = 56 * 1024 * 1024


def _resident(shape):
    zeros = (0,) * len(shape)
    return pl.BlockSpec(shape, lambda *_: zeros, pipeline_mode=pl.Buffered(1))


def _rms(x, g):
    ms = jnp.mean(x * x, axis=-1, keepdims=True)
    return x * lax.rsqrt(ms + EPS) * g


def _ffn_kernel(x_ref, pre_ref, w1_ref, w3_ref, w2_ref, post_ref, mixg_ref, o_ref, *rest, emit_u):
    if emit_u:
        u_ref, h_scr = rest
    else:
        (h_scr,) = rest
    x = x_ref[...]
    xn = _rms(x, pre_ref[...]).astype(BF16)
    for c in range(D_FF // FFN_CK):
        cols = slice(c * FFN_CK, (c + 1) * FFN_CK)
        a = jnp.dot(xn, w1_ref[:, cols], preferred_element_type=F32)
        b = jnp.dot(xn, w3_ref[:, cols], preferred_element_type=F32)
        h = a * (1.0 / (1.0 + jnp.exp(-a))) * b
        h_scr[:, cols] = h.astype(BF16)
    y = jnp.dot(h_scr[...], w2_ref[...], preferred_element_type=F32)
    xo = x + _rms(y, post_ref[...])
    o_ref[...] = xo
    if emit_u:
        u_ref[...] = _rms(xo, mixg_ref[...]).astype(BF16)


def _ffn(x2d, pre, w1, w3, w2, post, mixg, emit_u):
    n = x2d.shape[0]
    tm = FFN_TM
    row = lambda i: (i, 0)
    out_shape = [jax.ShapeDtypeStruct((n, D_MODEL), F32)]
    out_specs = [pl.BlockSpec((tm, D_MODEL), row)]
    if emit_u:
        out_shape.append(jax.ShapeDtypeStruct((n, D_MODEL), BF16))
        out_specs.append(pl.BlockSpec((tm, D_MODEL), row))
    res = pl.pallas_call(
        functools.partial(_ffn_kernel, emit_u=emit_u),
        grid=(n // tm,),
        in_specs=[
            pl.BlockSpec((tm, D_MODEL), row),
            _resident((1, D_MODEL)),
            _resident(w1.shape),
            _resident(w3.shape),
            _resident(w2.shape),
            _resident((1, D_MODEL)),
            _resident((1, D_MODEL)),
        ],
        out_specs=out_specs,
        out_shape=out_shape,
        scratch_shapes=[pltpu.VMEM((tm, D_FF), BF16)],
        compiler_params=pltpu.CompilerParams(
            dimension_semantics=("arbitrary",), vmem_limit_bytes=_VMEM_LIMIT),
        name="ffn_u" if emit_u else "ffn",
    )(x2d, pre, w1, w3, w2, post, mixg)
    return res


def _pair_swap(x):
    lane = lax.broadcasted_iota(jnp.int32, x.shape, 1)
    nxt = pltpu.roll(x, LANES - 1, 1)
    prv = pltpu.roll(x, 1, 1)
    return jnp.where((lane & 1) == 0, nxt, prv)


def _norm_rope(r, ssum_ref, g_ref, cos, sin, scale):
    r2 = (r * r).astype(BF16)
    ss = jnp.concatenate(
        [jnp.dot(r2[:, j:j + GROUP_W], ssum_ref[...], preferred_element_type=F32)
         for j in range(0, r.shape[1], GROUP_W)], axis=1)
    rinv = lax.rsqrt(ss * (1.0 / HEAD_DIM) + EPS) * scale
    xg = r * g_ref[...]
    outs = []
    for j in range(r.shape[1] // LANES):
        xs = xg[:, j * LANES:(j + 1) * LANES]
        outs.append(xs * cos + _pair_swap(xs) * sin)
    return jnp.concatenate(outs, axis=1) * rinv


def _group_tile(x, g):
    pair = x[:, (g // 2) * LANES:(g // 2 + 1) * LANES]
    rot = pltpu.roll(pair, HEAD_DIM, 1)
    lane = lax.broadcasted_iota(jnp.int32, pair.shape, 1)
    lo = lane < HEAD_DIM
    t = jnp.where(lo, pair, rot) if g % 2 == 0 else jnp.where(lo, rot, pair)
    return jnp.concatenate([t, t], axis=1)


def _inproj_kernel(u_ref, mem_ref, win_ref, wmkv_ref,
                   convw_ref, convb_ref, qg_ref, kg_ref, memg_ref, cos_ref, sin_ref, hsum_ref,
                   cbz_ref, q_ref, kt_ref, vt_ref, qm_ref, km_ref, vm_ref,
                   z_scr, cb_scr):
    wc_ref = win_ref.at[:, W_IN_Q - 3 * D_CONV:W_IN_Q]
    wq_ref = win_ref.at[:, W_IN_Q:W_IN_K]
    wkv_ref = win_ref.at[:, W_IN_K:W_IN_QM]
    wqm_ref = win_ref.at[:, W_IN_QM:W_IN_G]
    t_len = cbz_ref.shape[0]
    n_chunks = t_len // ROW_CHUNK
    j = pl.program_id(1)

    @pl.when(j == 0)
    def _():
        mn = _rms(mem_ref[...], memg_ref[...]).astype(BF16)
        kv = jnp.dot(mn, wmkv_ref[...], preferred_element_type=F32)
        km_ref[...] = kv[:, :D_MEM].astype(BF16)
        vm_ref[...] = kv[:, D_MEM:].astype(BF16)
        zero_rows = jnp.zeros((8, D_CONV), F32)
        z_scr[0:8, :] = zero_rows
        z_scr[t_len + 8:t_len + 16, :] = zero_rows

    r0 = pl.multiple_of(j * ROW_CHUNK, ROW_CHUNK)
    uc = u_ref[...]

    rc = jnp.dot(uc, wc_ref[...], preferred_element_type=F32)
    z_scr[pl.ds(r0 + 8, ROW_CHUNK), :] = rc[:, 2 * D_CONV:] * rc[:, :D_CONV]
    cb_scr[pl.ds(r0, ROW_CHUNK), :] = rc[:, D_CONV:2 * D_CONV].astype(BF16)

    cos = cos_ref[pl.ds(r0, ROW_CHUNK), :]
    sin = sin_ref[pl.ds(r0, ROW_CHUNK), :]
    rq = jnp.dot(uc, wq_ref[...], preferred_element_type=F32)
    q_ref[...] = _norm_rope(rq, hsum_ref, qg_ref, cos, sin, HEAD_DIM ** -0.5 * LOG2_E).astype(BF16)

    rkv = jnp.dot(uc, wkv_ref[...], preferred_element_type=F32)
    kk = _norm_rope(rkv[:, :GROUP_W], hsum_ref, kg_ref, cos, sin, 1.0)
    vvt = rkv[:, GROUP_W:].T
    tail = (lax.broadcasted_iota(jnp.int32, (V_ROWS - HEAD_DIM, ROW_CHUNK), 0) == 0).astype(BF16)
    for g in range(N_KV_HEADS):
        kt_ref[g] = _group_tile(kk, g).astype(BF16)
        vt_ref[g, 0:HEAD_DIM, :] = vvt[g * HEAD_DIM:(g + 1) * HEAD_DIM, :].astype(BF16)
        vt_ref[g, HEAD_DIM:V_ROWS, :] = tail

    rqm = jnp.dot(uc, wqm_ref[...], preferred_element_type=F32)
    qm_ref[...] = (rqm * (MEM_HEAD_DIM ** -0.5)).astype(BF16)

    @pl.when(j == n_chunks - 1)
    def _():
        cw = convw_ref[...]
        for c in range(n_chunks):
            c0 = c * ROW_CHUNK
            zc = (z_scr[c0 + 7:c0 + 7 + ROW_CHUNK, :] * cw[0:1, :]
                  + z_scr[c0 + 8:c0 + 8 + ROW_CHUNK, :] * cw[1:2, :]
                  + z_scr[c0 + 9:c0 + 9 + ROW_CHUNK, :] * cw[2:3, :]
                  + convb_ref[...])
            cbz_ref[c0:c0 + ROW_CHUNK, :] = (cb_scr[c0:c0 + ROW_CHUNK, :].astype(F32) * zc).astype(BF16)


def _inproj(u, mem, w, tables):
    b, t, _ = u.shape
    cos, sin, hsum = tables
    rows = lambda width: pl.BlockSpec((None, ROW_CHUNK, width), lambda i, j: (i, j, 0))
    tiled = pl.BlockSpec((None, N_KV_HEADS, ROW_CHUNK, GROUP_W), lambda i, j: (i, 0, j, 0))
    vspec = pl.BlockSpec((None, N_KV_HEADS, V_ROWS, ROW_CHUNK), lambda i, j: (i, 0, 0, j))
    memspec = pl.BlockSpec((None, N_MEM, D_MEM), lambda i, j: (i, 0, 0))
    return pl.pallas_call(
        _inproj_kernel,
        grid=(b, t // ROW_CHUNK),
        in_specs=[
            rows(D_MODEL),
            pl.BlockSpec((None, N_MEM, D_MODEL), lambda i, j: (i, 0, 0)),
            _resident(w["w_in"].shape), _resident(w["w_mkv"].shape),
            _resident((3, D_CONV)), _resident((1, D_CONV)),
            _resident((1, N_HEADS * HEAD_DIM)), _resident((1, GROUP_W)), _resident((1, D_MODEL)),
            _resident(cos.shape), _resident(sin.shape), _resident(hsum.shape),
        ],
        out_specs=[pl.BlockSpec((None, t, D_CONV), lambda i, j: (i, 0, 0)), rows(N_HEADS * HEAD_DIM),
                   tiled, vspec, rows(D_MEM), memspec, memspec],
        out_shape=[
            jax.ShapeDtypeStruct((b, t, D_CONV), BF16),
            jax.ShapeDtypeStruct((b, t, N_HEADS * HEAD_DIM), BF16),
            jax.ShapeDtypeStruct((b, N_KV_HEADS, t, GROUP_W), BF16),
            jax.ShapeDtypeStruct((b, N_KV_HEADS, V_ROWS, t), BF16),
            jax.ShapeDtypeStruct((b, t, D_MEM), BF16),
            jax.ShapeDtypeStruct((b, N_MEM, D_MEM), BF16),
            jax.ShapeDtypeStruct((b, N_MEM, D_MEM), BF16),
        ],
        scratch_shapes=[pltpu.VMEM((t + 16, D_CONV), F32), pltpu.VMEM((t, D_CONV), BF16)],
        compiler_params=pltpu.CompilerParams(
            dimension_semantics=("arbitrary", "arbitrary"), vmem_limit_bytes=_VMEM_LIMIT),
        name="inproj",
    )(u, mem, w["w_in"], w["w_mkv"],
      w["conv_w"], w["conv_b"], w["q_gain"], w["k_gain"], w["mem_norm"], cos, sin, hsum)


def _attn_step(q, kt_ref, vt_ref, s_new, m_new, s_cur, m_cur):
    tq = q.shape[0]
    t = s_new.shape[0]
    cols = GROUP * tq
    qf = q.astype(F32)
    head = lax.broadcasted_iota(jnp.int32, qf.shape, 1) // HEAD_DIM
    qs = jnp.concatenate([jnp.where(head == h, qf, 0.0) for h in range(GROUP)], axis=0).astype(BF16)
    st = lax.dot_general(kt_ref[...], qs, (((1,), (1,)), ((), ())), preferred_element_type=F32)
    s_new[:, 0:cols] = st
    m_new[...] = jnp.max(st, axis=0, keepdims=True)

    m = m_cur[...]
    accs = []
    for c in range(cols // MXU_N):
        csl = slice(c * MXU_N, (c + 1) * MXU_N)
        acc = None
        for k in range(t // MXU_N):
            ksl = slice(k * MXU_N, (k + 1) * MXU_N)
            p = jnp.exp2(s_cur[ksl, csl] - m[:, csl]).astype(BF16)
            d = jnp.dot(vt_ref[:, ksl], p, preferred_element_type=F32)
            acc = d if acc is None else acc + d
        accs.append(acc)
    ot = jnp.concatenate(accs, axis=1)
    on = ot[0:HEAD_DIM, :] * (1.0 / ot[HEAD_DIM:HEAD_DIM + 1, :])
    out_t = jnp.concatenate([on[:, h * tq:(h + 1) * tq] for h in range(GROUP)], axis=0)
    return out_t.T.astype(BF16)


def _attn_kernel(q_ref, kt_ref, vt_ref, o_ref, s0, s1, m0, m1):
    n = pl.program_id(0)

    @pl.when(n == 0)
    def _():
        s1[...] = jnp.zeros_like(s1)
        m1[...] = jnp.zeros_like(m1)

    @pl.when(n % 2 == 0)
    def _():
        o_ref[...] = _attn_step(q_ref[...], kt_ref, vt_ref, s0, m0, s1, m1)

    @pl.when(n % 2 == 1)
    def _():
        o_ref[...] = _attn_step(q_ref[...], kt_ref, vt_ref, s1, m1, s0, m0)


def _attn(q, kt, vt):
    b, t, _ = q.shape
    tq = ATTN_TQ
    tiles_per_seq = t // tq
    n_tiles = b * N_KV_HEADS * tiles_per_seq

    def place(i):
        return i // (N_KV_HEADS * tiles_per_seq), (i // tiles_per_seq) % N_KV_HEADS, i % tiles_per_seq

    def score_tile(n):
        return place(jnp.minimum(n, n_tiles - 1))

    def value_tile(n):
        return place(jnp.maximum(n - 1, 0))

    def q_map(n):
        bi, g, j = score_tile(n)
        return bi, j, g

    def kt_map(n):
        bi, g, _ = score_tile(n)
        return bi, g, 0, 0

    def vt_map(n):
        bi, g, _ = value_tile(n)
        return bi, g, 0, 0

    def o_map(n):
        bi, g, j = value_tile(n)
        return bi, j, g

    cols = GROUP * tq
    return pl.pallas_call(
        _attn_kernel,
        grid=(n_tiles + 1,),
        in_specs=[pl.BlockSpec((None, tq, GROUP_W), q_map),
                  pl.BlockSpec((None, None, t, GROUP_W), kt_map),
                  pl.BlockSpec((None, None, V_ROWS, t), vt_map)],
        out_specs=pl.BlockSpec((None, tq, GROUP_W), o_map),
        out_shape=jax.ShapeDtypeStruct((b, t, N_HEADS * HEAD_DIM), BF16),
        scratch_shapes=[pltpu.VMEM((t, cols + LANES), F32), pltpu.VMEM((t, cols + LANES), F32),
                        pltpu.VMEM((1, cols), F32), pltpu.VMEM((1, cols), F32)],
        compiler_params=pltpu.CompilerParams(
            dimension_semantics=("arbitrary",), vmem_limit_bytes=_VMEM_LIMIT),
        name="attn",
    )(q, kt, vt)


def _merge_kernel(x_ref, cbz_ref, attn_ref, qm_ref, km_ref, vm_ref,
                  pconv_ref, pattn_ref, pmem_ref, wg_ref, bg_ref, wout_ref, mixg_ref, postg_ref, o_ref):
    x = x_ref[...]
    u = _rms(x, mixg_ref[...]).astype(BF16)

    qm = qm_ref[...]
    heads = []
    for h in range(N_MEM_HEADS):
        cols = slice(h * MEM_HEAD_DIM, (h + 1) * MEM_HEAD_DIM)
        s = lax.dot_general(qm[:, cols], km_ref[:, cols], (((1,), (1,)), ((), ())),
                            preferred_element_type=F32)
        p = jnp.exp(s - jnp.max(s, axis=-1, keepdims=True))
        l = jnp.sum(p, axis=-1, keepdims=True)
        oh = jnp.dot(p.astype(BF16), vm_ref[:, cols], preferred_element_type=F32)
        heads.append(oh * (1.0 / l))
    memo = jnp.concatenate(heads, axis=1).astype(BF16)

    branches = (
        (cbz_ref[...], pconv_ref),
        (attn_ref[...], pattn_ref),
        (memo, pmem_ref),
    )
    merged = None
    for br, (act, proj_ref) in enumerate(branches):
        cols = slice(br * D_MODEL, (br + 1) * D_MODEL)
        gate = jnp.dot(u, wg_ref[:, cols], preferred_element_type=F32) + bg_ref[:, cols]
        gate = 1.0 / (1.0 + jnp.exp(-gate))
        term = gate * jnp.dot(act, proj_ref[...], preferred_element_type=F32)
        merged = term if merged is None else merged + term
    y = jnp.dot(merged.astype(BF16), wout_ref[...], preferred_element_type=F32)
    o_ref[...] = x + _rms(y, postg_ref[...])


def _merge(x, cbz, attn, qm, km, vm, w):
    b, t, _ = x.shape
    tm = MERGE_TM
    tok = lambda width: pl.BlockSpec((None, tm, width), lambda i, j: (i, j, 0))
    memspec = pl.BlockSpec((None, N_MEM, D_MEM), lambda i, j: (i, 0, 0))
    return pl.pallas_call(
        _merge_kernel,
        grid=(b, t // tm),
        in_specs=[
            tok(D_MODEL), tok(D_CONV), tok(N_HEADS * HEAD_DIM), tok(D_MEM), memspec, memspec,
            _resident(w["p_conv"].shape), _resident(w["p_attn"].shape), _resident(w["p_mem"].shape),
            _resident(w["w_gate"].shape), _resident((1, N_BRANCH * D_MODEL)), _resident(w["w_out"].shape),
            _resident((1, D_MODEL)), _resident((1, D_MODEL)),
        ],
        out_specs=tok(D_MODEL),
        out_shape=jax.ShapeDtypeStruct((b, t, D_MODEL), F32),
        compiler_params=pltpu.CompilerParams(
            dimension_semantics=("arbitrary", "arbitrary"), vmem_limit_bytes=_VMEM_LIMIT),
        name="merge",
    )(x, cbz, attn, qm, km, vm, w["p_conv"], w["p_attn"], w["p_mem"], w["w_gate"], w["b_gate"],
      w["w_out"], w["mix_pre"], w["mix_post"])


def _rope_tables(t):
    pos = np.arange(t)
    row = (pos // GRID_W).astype(np.float64)
    col = (pos % GRID_W).astype(np.float64)
    inv = 1.0 / (ROPE_THETA ** (np.arange(0, AXIS_DIM, 2, dtype=np.float64) / AXIS_DIM))
    ang = np.concatenate([row[:, None] * inv, col[:, None] * inv], axis=-1)
    cos = np.repeat(np.cos(ang), 2, axis=-1)
    sin = np.stack([-np.sin(ang), np.sin(ang)], axis=-1).reshape(t, HEAD_DIM)
    reps = LANES // HEAD_DIM
    return (jnp.asarray(np.tile(cos, (1, reps)), dtype=F32), jnp.asarray(np.tile(sin, (1, reps)), dtype=F32))


def _head_sum_matrix(width):
    head = np.arange(width) // HEAD_DIM
    return jnp.asarray(head[:, None] == head[None, :], dtype=BF16)


def _layer(x, mem, w, tables):
    b, t, d = x.shape
    x1, u = _ffn(x.reshape(b * t, d), w["ffn1_pre"], w["ffn1_w1"], w["ffn1_w3"], w["ffn1_w2"], w["ffn1_post"],
                 w["mix_pre"], emit_u=True)
    cbz, q, kt, vt, qm, km, vm = _inproj(u.reshape(b, t, d), mem, w, tables)
    attn = _attn(q, kt, vt)
    x2 = _merge(x1.reshape(b, t, d), cbz, attn, qm, km, vm, w)
    (y,) = _ffn(x2.reshape(b * t, d), w["ffn2_pre"], w["ffn2_w1"], w["ffn2_w3"], w["ffn2_w2"], w["ffn2_post"],
                w["mix_pre"], emit_u=False)
    return y.reshape(b, t, d)


def kernel(x_prompt, x_sample, mem_prompt, mem_sample, ffn1_pre, ffn1_w1, ffn1_w3, ffn1_w2, ffn1_post, mix_pre, w_in, conv_w, conv_b, p_conv, q_norm, k_norm, p_attn, mem_norm, w_mem_kv, p_mem, b_gate, w_out, mix_post, ffn2_pre, ffn2_w1, ffn2_w3, ffn2_w2, ffn2_post):
    y_prompt, y_sample = x_prompt, x_sample
    for l in range(ffn1_pre.shape[0]):
        w = {
            "ffn1_pre": ffn1_pre[l][None], "ffn1_post": 0.5 * ffn1_post[l][None],
            "ffn1_w1": ffn1_w1[l].astype(BF16), "ffn1_w3": ffn1_w3[l].astype(BF16),
            "ffn1_w2": ffn1_w2[l].astype(BF16),
            "ffn2_pre": ffn2_pre[l][None], "ffn2_post": 0.5 * ffn2_post[l][None],
            "ffn2_w1": ffn2_w1[l].astype(BF16), "ffn2_w3": ffn2_w3[l].astype(BF16),
            "ffn2_w2": ffn2_w2[l].astype(BF16),
            "mix_pre": mix_pre[l][None], "mix_post": mix_post[l][None],
            "w_in": w_in[l, :, :W_IN_G].astype(BF16), "w_gate": w_in[l, :, W_IN_G:].astype(BF16),
            "b_gate": b_gate[l][None],
            "conv_w": conv_w[l], "conv_b": conv_b[l][None],
            "q_gain": jnp.tile(q_norm[l], N_HEADS)[None], "k_gain": jnp.tile(k_norm[l], N_KV_HEADS)[None],
            "mem_norm": mem_norm[l][None], "w_mkv": w_mem_kv[l].astype(BF16),
            "p_conv": p_conv[l].astype(BF16), "p_attn": p_attn[l].astype(BF16), "p_mem": p_mem[l].astype(BF16),
            "w_out": w_out[l].astype(BF16),
        }
        hsum = _head_sum_matrix(GROUP_W)
        y_prompt = _layer(y_prompt, mem_prompt, w, _rope_tables(y_prompt.shape[1]) + (hsum,))
        y_sample = _layer(y_sample, mem_sample, w, _rope_tables(y_sample.shape[1]) + (hsum,))
    return (y_prompt, y_sample)
```

```python
import functools

import jax
import jax.numpy as jnp
import numpy as np
from jax import lax
from jax.experimental import pallas as pl
from jax.experimental.pallas import tpu as pltpu

D_MODEL = 1024
D_CONV = 512
N_HEADS = 16
N_KV_HEADS = 4
GROUP = N_HEADS // N_KV_HEADS
HEAD_DIM = 64
AXIS_DIM = HEAD_DIM // 2
ROPE_THETA = 10000.0
GRID_W = 64
N_MEM = 256
N_MEM_HEADS = 4
MEM_HEAD_DIM = 128
N_BRANCH = 3
D_FF = 2816
EPS = 1e-6

LANES = 128
MXU_N = 256
GROUP_W = GROUP * HEAD_DIM
D_MEM = N_MEM_HEADS * MEM_HEAD_DIM
W_IN_Q = 3 * D_CONV
W_IN_K = W_IN_Q + N_HEADS * HEAD_DIM
W_IN_QM = W_IN_K + 2 * N_KV_HEADS * HEAD_DIM
W_IN_G = W_IN_QM + D_MEM

FFN_TM = 1024
MERGE_TM = 1024
FFN_CK = MXU_N
ROW_CHUNK = 1024
ATTN_TQ = 512
V_ROWS = MXU_N
LOG2_E = 1.4426950408889634

F32 = jnp.float32
BF16 = jnp.bfloat16

_VMEM_LIMIT = 56 * 1024 * 1024


def _resident(shape):
    zeros = (0,) * len(shape)
    return pl.BlockSpec(shape, lambda *_: zeros, pipeline_mode=pl.Buffered(1))


def _rms(x, g):
    ms = jnp.mean(x * x, axis=-1, keepdims=True)
    return x * lax.rsqrt(ms + EPS) * g


def _ffn_kernel(x_ref, pre_ref, w1_ref, w3_ref, w2_ref, post_ref, mixg_ref, o_ref, *rest, emit_u):
    if emit_u:
        u_ref, h_scr = rest
    else:
        (h_scr,) = rest
    x = x_ref[...]
    xn = _rms(x, pre_ref[...]).astype(BF16)
    for c in range(D_FF // FFN_CK):
        cols = slice(c * FFN_CK, (c + 1) * FFN_CK)
        a = jnp.dot(xn, w1_ref[:, cols], preferred_element_type=F32)
        b = jnp.dot(xn, w3_ref[:, cols], preferred_element_type=F32)
        h = a * (1.0 / (1.0 + jnp.exp(-a))) * b
        h_scr[:, cols] = h.astype(BF16)
    y = jnp.dot(h_scr[...], w2_ref[...], preferred_element_type=F32)
    xo = x + _rms(y, post_ref[...])
    o_ref[...] = xo
    if emit_u:
        u_ref[...] = _rms(xo, mixg_ref[...]).astype(BF16)


def _ffn(x2d, pre, w1, w3, w2, post, mixg, emit_u):
    n = x2d.shape[0]
    tm = FFN_TM
    row = lambda i: (i, 0)
    out_shape = [jax.ShapeDtypeStruct((n, D_MODEL), F32)]
    out_specs = [pl.BlockSpec((tm, D_MODEL), row)]
    if emit_u:
        out_shape.append(jax.ShapeDtypeStruct((n, D_MODEL), BF16))
        out_specs.append(pl.BlockSpec((tm, D_MODEL), row))
    res = pl.pallas_call(
        functools.partial(_ffn_kernel, emit_u=emit_u),
        grid=(n // tm,),
        in_specs=[
            pl.BlockSpec((tm, D_MODEL), row),
            _resident((1, D_MODEL)),
            _resident(w1.shape),
            _resident(w3.shape),
            _resident(w2.shape),
            _resident((1, D_MODEL)),
            _resident((1, D_MODEL)),
        ],
        out_specs=out_specs,
        out_shape=out_shape,
        scratch_shapes=[pltpu.VMEM((tm, D_FF), BF16)],
        compiler_params=pltpu.CompilerParams(
            dimension_semantics=("arbitrary",), vmem_limit_bytes=_VMEM_LIMIT),
        name="ffn_u" if emit_u else "ffn",
    )(x2d, pre, w1, w3, w2, post, mixg)
    return res


def _pair_swap(x):
    lane = lax.broadcasted_iota(jnp.int32, x.shape, 1)
    nxt = pltpu.roll(x, LANES - 1, 1)
    prv = pltpu.roll(x, 1, 1)
    return jnp.where((lane & 1) == 0, nxt, prv)


def _norm_rope(r, ssum_ref, g_ref, cos, sin, scale):
    r2 = (r * r).astype(BF16)
    ss = jnp.concatenate(
        [jnp.dot(r2[:, j:j + GROUP_W], ssum_ref[...], preferred_element_type=F32)
         for j in range(0, r.shape[1], GROUP_W)], axis=1)
    rinv = lax.rsqrt(ss * (1.0 / HEAD_DIM) + EPS) * scale
    xg = r * g_ref[...]
    outs = []
    for j in range(r.shape[1] // LANES):
        xs = xg[:, j * LANES:(j + 1) * LANES]
        outs.append(xs * cos + _pair_swap(xs) * sin)
    return jnp.concatenate(outs, axis=1) * rinv


def _group_tile(x, g):
    pair = x[:, (g // 2) * LANES:(g // 2 + 1) * LANES]
    rot = pltpu.roll(pair, HEAD_DIM, 1)
    lane = lax.broadcasted_iota(jnp.int32, pair.shape, 1)
    lo = lane < HEAD_DIM
    t = jnp.where(lo, pair, rot) if g % 2 == 0 else jnp.where(lo, rot, pair)
    return jnp.concatenate([t, t], axis=1)


def _inproj_kernel(u_ref, mem_ref, win_ref, wmkv_ref,
                   convw_ref, convb_ref, qg_ref, kg_ref, memg_ref, cos_ref, sin_ref, hsum_ref,
                   cbz_ref, q_ref, kt_ref, vt_ref, qm_ref, km_ref, vm_ref,
                   z_scr, cb_scr):
    wc_ref = win_ref.at[:, W_IN_Q - 3 * D_CONV:W_IN_Q]
    wq_ref = win_ref.at[:, W_IN_Q:W_IN_K]
    wkv_ref = win_ref.at[:, W_IN_K:W_IN_QM]
    wqm_ref = win_ref.at[:, W_IN_QM:W_IN_G]
    t_len = cbz_ref.shape[0]
    n_chunks = t_len // ROW_CHUNK
    j = pl.program_id(1)

    @pl.when(j == 0)
    def _():
        mn = _rms(mem_ref[...], memg_ref[...]).astype(BF16)
        kv = jnp.dot(mn, wmkv_ref[...], preferred_element_type=F32)
        km_ref[...] = kv[:, :D_MEM].astype(BF16)
        vm_ref[...] = kv[:, D_MEM:].astype(BF16)
        zero_rows = jnp.zeros((8, D_CONV), F32)
        z_scr[0:8, :] = zero_rows
        z_scr[t_len + 8:t_len + 16, :] = zero_rows

    r0 = pl.multiple_of(j * ROW_CHUNK, ROW_CHUNK)
    uc = u_ref[...]

    rc = jnp.dot(uc, wc_ref[...], preferred_element_type=F32)
    z_scr[pl.ds(r0 + 8, ROW_CHUNK), :] = rc[:, 2 * D_CONV:] * rc[:, :D_CONV]
    cb_scr[pl.ds(r0, ROW_CHUNK), :] = rc[:, D_CONV:2 * D_CONV].astype(BF16)

    cos = cos_ref[pl.ds(r0, ROW_CHUNK), :]
    sin = sin_ref[pl.ds(r0, ROW_CHUNK), :]
    rq = jnp.dot(uc, wq_ref[...], preferred_element_type=F32)
    q_ref[...] = _norm_rope(rq, hsum_ref, qg_ref, cos, sin, HEAD_DIM ** -0.5 * LOG2_E).astype(BF16)

    rkv = jnp.dot(uc, wkv_ref[...], preferred_element_type=F32)
    kk = _norm_rope(rkv[:, :GROUP_W], hsum_ref, kg_ref, cos, sin, 1.0)
    vvt = rkv[:, GROUP_W:].T
    tail = (lax.broadcasted_iota(jnp.int32, (V_ROWS - HEAD_DIM, ROW_CHUNK), 0) == 0).astype(BF16)
    for g in range(N_KV_HEADS):
        kt_ref[g] = _group_tile(kk, g).astype(BF16)
        vt_ref[g, 0:HEAD_DIM, :] = vvt[g * HEAD_DIM:(g + 1) * HEAD_DIM, :].astype(BF16)
        vt_ref[g, HEAD_DIM:V_ROWS, :] = tail

    rqm = jnp.dot(uc, wqm_ref[...], preferred_element_type=F32)
    qm_ref[...] = (rqm * (MEM_HEAD_DIM ** -0.5)).astype(BF16)

    @pl.when(j == n_chunks - 1)
    def _():
        cw = convw_ref[...]
        for c in range(n_chunks):
            c0 = c * ROW_CHUNK
            zc = (z_scr[c0 + 7:c0 + 7 + ROW_CHUNK, :] * cw[0:1, :]
                  + z_scr[c0 + 8:c0 + 8 + ROW_CHUNK, :] * cw[1:2, :]
                  + z_scr[c0 + 9:c0 + 9 + ROW_CHUNK, :] * cw[2:3, :]
                  + convb_ref[...])
            cbz_ref[c0:c0 + ROW_CHUNK, :] = (cb_scr[c0:c0 + ROW_CHUNK, :].astype(F32) * zc).astype(BF16)


def _inproj(u, mem, w, tables):
    b, t, _ = u.shape
    cos, sin, hsum = tables
    rows = lambda width: pl.BlockSpec((None, ROW_CHUNK, width), lambda i, j: (i, j, 0))
    tiled = pl.BlockSpec((None, N_KV_HEADS, ROW_CHUNK, GROUP_W), lambda i, j: (i, 0, j, 0))
    vspec = pl.BlockSpec((None, N_KV_HEADS, V_ROWS, ROW_CHUNK), lambda i, j: (i, 0, 0, j))
    memspec = pl.BlockSpec((None, N_MEM, D_MEM), lambda i, j: (i, 0, 0))
    return pl.pallas_call(
        _inproj_kernel,
        grid=(b, t // ROW_CHUNK),
        in_specs=[
            rows(D_MODEL),
            pl.BlockSpec((None, N_MEM, D_MODEL), lambda i, j: (i, 0, 0)),
            _resident(w["w_in"].shape), _resident(w["w_mkv"].shape),
            _resident((3, D_CONV)), _resident((1, D_CONV)),
            _resident((1, N_HEADS * HEAD_DIM)), _resident((1, GROUP_W)), _resident((1, D_MODEL)),
            _resident(cos.shape), _resident(sin.shape), _resident(hsum.shape),
        ],
        out_specs=[pl.BlockSpec((None, t, D_CONV), lambda i, j: (i, 0, 0)), rows(N_HEADS * HEAD_DIM),
                   tiled, vspec, rows(D_MEM), memspec, memspec],
        out_shape=[
            jax.ShapeDtypeStruct((b, t, D_CONV), BF16),
            jax.ShapeDtypeStruct((b, t, N_HEADS * HEAD_DIM), BF16),
            jax.ShapeDtypeStruct((b, N_KV_HEADS, t, GROUP_W), BF16),
            jax.ShapeDtypeStruct((b, N_KV_HEADS, V_ROWS, t), BF16),
            jax.ShapeDtypeStruct((b, t, D_MEM), BF16),
            jax.ShapeDtypeStruct((b, N_MEM, D_MEM), BF16),
            jax.ShapeDtypeStruct((b, N_MEM, D_MEM), BF16),
        ],
        scratch_shapes=[pltpu.VMEM((t + 16, D_CONV), F32), pltpu.VMEM((t, D_CONV), BF16)],
        compiler_params=pltpu.CompilerParams(
            dimension_semantics=("arbitrary", "arbitrary"), vmem_limit_bytes=_VMEM_LIMIT),
        name="inproj",
    )(u, mem, w["w_in"], w["w_mkv"],
      w["conv_w"], w["conv_b"], w["q_gain"], w["k_gain"], w["mem_norm"], cos, sin, hsum)


def _attn_step(q, kt_ref, vt_ref, s_new, m_new, s_cur, m_cur):
    tq = q.shape[0]
    t = s_new.shape[0]
    cols = GROUP * tq
    qf = q.astype(F32)
    head = lax.broadcasted_iota(jnp.int32, qf.shape, 1) // HEAD_DIM
    qs = jnp.concatenate([jnp.where(head == h, qf, 0.0) for h in range(GROUP)], axis=0).astype(BF16)
    st = lax.dot_general(kt_ref[...], qs, (((1,), (1,)), ((), ())), preferred_element_type=F32)
    s_new[:, 0:cols] = st
    m_new[...] = jnp.max(st, axis=0, keepdims=True)

    m = m_cur[...]
    accs = []
    for c in range(cols // MXU_N):
        csl = slice(c * MXU_N, (c + 1) * MXU_N)
        acc = None
        for k in range(t // MXU_N):
            ksl = slice(k * MXU_N, (k + 1) * MXU_N)
            p = jnp.exp2(s_cur[ksl, csl] - m[:, csl]).astype(BF16)
            d = jnp.dot(vt_ref[:, ksl], p, preferred_element_type=F32)
            acc = d if acc is None else acc + d
        accs.append(acc)
    ot = jnp.concatenate(accs, axis=1)
    on = ot[0:HEAD_DIM, :] * (1.0 / ot[HEAD_DIM:HEAD_DIM + 1, :])
    out_t = jnp.concatenate([on[:, h * tq:(h + 1) * tq] for h in range(GROUP)], axis=0)
    return out_t.T.astype(BF16)


def _attn_kernel(q_ref, kt_ref, vt_ref, o_ref, s0, s1, m0, m1):
    n = pl.program_id(0)

    @pl.when(n == 0)
    def _():
        s1[...] = jnp.zeros_like(s1)
        m1[...] = jnp.zeros_like(m1)

    @pl.when(n % 2 == 0)
    def _():
        o_ref[...] = _attn_step(q_ref[...], kt_ref, vt_ref, s0, m0, s1, m1)

    @pl.when(n % 2 == 1)
    def _():
        o_ref[...] = _attn_step(q_ref[...], kt_ref, vt_ref, s1, m1, s0, m0)


def _attn(q, kt, vt):
    b, t, _ = q.shape
    tq = ATTN_TQ
    tiles_per_seq = t // tq
    n_tiles = b * N_KV_HEADS * tiles_per_seq

    def place(i):
        return i // (N_KV_HEADS * tiles_per_seq), (i // tiles_per_seq) % N_KV_HEADS, i % tiles_per_seq

    def score_tile(n):
        return place(jnp.minimum(n, n_tiles - 1))

    def value_tile(n):
        return place(jnp.maximum(n - 1, 0))

    def q_map(n):
        bi, g, j = score_tile(n)
        return bi, j, g

    def kt_map(n):
        bi, g, _ = score_tile(n)
        return bi, g, 0, 0

    def vt_map(n):
        bi, g, _ = value_tile(n)
        return bi, g, 0, 0

    def o_map(n):
        bi, g, j = value_tile(n)
        return bi, j, g

    cols = GROUP * tq
    return pl.pallas_call(
        _attn_kernel,
        grid=(n_tiles + 1,),
        in_specs=[pl.BlockSpec((None, tq, GROUP_W), q_map),
                  pl.BlockSpec((None, None, t, GROUP_W), kt_map),
                  pl.BlockSpec((None, None, V_ROWS, t), vt_map)],
        out_specs=pl.BlockSpec((None, tq, GROUP_W), o_map),
        out_shape=jax.ShapeDtypeStruct((b, t, N_HEADS * HEAD_DIM), BF16),
        scratch_shapes=[pltpu.VMEM((t, cols + LANES), F32), pltpu.VMEM((t, cols + LANES), F32),
                        pltpu.VMEM((1, cols), F32), pltpu.VMEM((1, cols), F32)],
        compiler_params=pltpu.CompilerParams(
            dimension_semantics=("arbitrary",), vmem_limit_bytes=_VMEM_LIMIT),
        name="attn",
    )(q, kt, vt)


def _merge_kernel(x_ref, cbz_ref, attn_ref, qm_ref, km_ref, vm_ref,
                  pconv_ref, pattn_ref, pmem_ref, wg_ref, bg_ref, wout_ref, mixg_ref, postg_ref, o_ref):
    x = x_ref[...]
    u = _rms(x, mixg_ref[...]).astype(BF16)

    qm = qm_ref[...]
    heads = []
    for h in range(N_MEM_HEADS):
        cols = slice(h * MEM_HEAD_DIM, (h + 1) * MEM_HEAD_DIM)
        s = lax.dot_general(qm[:, cols], km_ref[:, cols], (((1,), (1,)), ((), ())),
                            preferred_element_type=F32)
        p = jnp.exp(s - jnp.max(s, axis=-1, keepdims=True))
        l = jnp.sum(p, axis=-1, keepdims=True)
        oh = jnp.dot(p.astype(BF16), vm_ref[:, cols], preferred_element_type=F32)
        heads.append(oh * (1.0 / l))
    memo = jnp.concatenate(heads, axis=1).astype(BF16)

    branches = (
        (cbz_ref[...], pconv_ref),
        (attn_ref[...], pattn_ref),
        (memo, pmem_ref),
    )
    merged = None
    for br, (act, proj_ref) in enumerate(branches):
        cols = slice(br * D_MODEL, (br + 1) * D_MODEL)
        gate = jnp.dot(u, wg_ref[:, cols], preferred_element_type=F32) + bg_ref[:, cols]
        gate = 1.0 / (1.0 + jnp.exp(-gate))
        term = gate * jnp.dot(act, proj_ref[...], preferred_element_type=F32)
        merged = term if merged is None else merged + term
    y = jnp.dot(merged.astype(BF16), wout_ref[...], preferred_element_type=F32)
    o_ref[...] = x + _rms(y, postg_ref[...])


def _merge(x, cbz, attn, qm, km, vm, w):
    b, t, _ = x.shape
    tm = MERGE_TM
    tok = lambda width: pl.BlockSpec((None, tm, width), lambda i, j: (i, j, 0))
    memspec = pl.BlockSpec((None, N_MEM, D_MEM), lambda i, j: (i, 0, 0))
    return pl.pallas_call(
        _merge_kernel,
        grid=(b, t // tm),
        in_specs=[
            tok(D_MODEL), tok(D_CONV), tok(N_HEADS * HEAD_DIM), tok(D_MEM), memspec, memspec,
            _resident(w["p_conv"].shape), _resident(w["p_attn"].shape), _resident(w["p_mem"].shape),
            _resident(w["w_gate"].shape), _resident((1, N_BRANCH * D_MODEL)), _resident(w["w_out"].shape),
            _resident((1, D_MODEL)), _resident((1, D_MODEL)),
        ],
        out_specs=tok(D_MODEL),
        out_shape=jax.ShapeDtypeStruct((b, t, D_MODEL), F32),
        compiler_params=pltpu.CompilerParams(
            dimension_semantics=("arbitrary", "arbitrary"), vmem_limit_bytes=_VMEM_LIMIT),
        name="merge",
    )(x, cbz, attn, qm, km, vm, w["p_conv"], w["p_attn"], w["p_mem"], w["w_gate"], w["b_gate"],
      w["w_out"], w["mix_pre"], w["mix_post"])


def _rope_tables(t):
    pos = np.arange(t)
    row = (pos // GRID_W).astype(np.float64)
    col = (pos % GRID_W).astype(np.float64)
    inv = 1.0 / (ROPE_THETA ** (np.arange(0, AXIS_DIM, 2, dtype=np.float64) / AXIS_DIM))
    ang = np.concatenate([row[:, None] * inv, col[:, None] * inv], axis=-1)
    cos = np.repeat(np.cos(ang), 2, axis=-1)
    sin = np.stack([-np.sin(ang), np.sin(ang)], axis=-1).reshape(t, HEAD_DIM)
    reps = LANES // HEAD_DIM
    return (jnp.asarray(np.tile(cos, (1, reps)), dtype=F32), jnp.asarray(np.tile(sin, (1, reps)), dtype=F32))


def _head_sum_matrix(width):
    head = np.arange(width) // HEAD_DIM
    return jnp.asarray(head[:, None] == head[None, :], dtype=BF16)


def _layer(x, mem, w, tables):
    b, t, d = x.shape
    x1, u = _ffn(x.reshape(b * t, d), w["ffn1_pre"], w["ffn1_w1"], w["ffn1_w3"], w["ffn1_w2"], w["ffn1_post"],
                 w["mix_pre"], emit_u=True)
    cbz, q, kt, vt, qm, km, vm = _inproj(u.reshape(b, t, d), mem, w, tables)
    attn = _attn(q, kt, vt)
    x2 = _merge(x1.reshape(b, t, d), cbz, attn, qm, km, vm, w)
    (y,) = _ffn(x2.reshape(b * t, d), w["ffn2_pre"], w["ffn2_w1"], w["ffn2_w3"], w["ffn2_w2"], w["ffn2_post"],
                w["mix_pre"], emit_u=False)
    return y.reshape(b, t, d)


def kernel(x_prompt, x_sample, mem_prompt, mem_sample, ffn1_pre, ffn1_w1, ffn1_w3, ffn1_w2, ffn1_post, mix_pre, w_in, conv_w, conv_b, p_conv, q_norm, k_norm, p_attn, mem_norm, w_mem_kv, p_mem, b_gate, w_out, mix_post, ffn2_pre, ffn2_w1, ffn2_w3, ffn2_w2, ffn2_post):
    y_prompt, y_sample = x_prompt, x_sample
    for l in range(ffn1_pre.shape[0]):
        w = {
            "ffn1_pre": ffn1_pre[l][None], "ffn1_post": 0.5 * ffn1_post[l][None],
            "ffn1_w1": ffn1_w1[l].astype(BF16), "ffn1_w3": ffn1_w3[l].astype(BF16),
            "ffn1_w2": ffn1_w2[l].astype(BF16),
            "ffn2_pre": ffn2_pre[l][None], "ffn2_post": 0.5 * ffn2_post[l][None],
            "ffn2_w1": ffn2_w1[l].astype(BF16), "ffn2_w3": ffn2_w3[l].astype(BF16),
            "ffn2_w2": ffn2_w2[l].astype(BF16),
            "mix_pre": mix_pre[l][None], "mix_post": mix_post[l][None],
            "w_in": w_in[l, :, :W_IN_G].astype(BF16), "w_gate": w_in[l, :, W_IN_G:].astype(BF16),
            "b_gate": b_gate[l][None],
            "conv_w": conv_w[l], "conv_b": conv_b[l][None],
            "q_gain": jnp.tile(q_norm[l], N_HEADS)[None], "k_gain": jnp.tile(k_norm[l], N_KV_HEADS)[None],
            "mem_norm": mem_norm[l][None], "w_mkv": w_mem_kv[l].astype(BF16),
            "p_conv": p_conv[l].astype(BF16), "p_attn": p_attn[l].astype(BF16), "p_mem": p_mem[l].astype(BF16),
            "w_out": w_out[l].astype(BF16),
        }
        hsum = _head_sum_matrix(GROUP_W)
        y_prompt = _layer(y_prompt, mem_prompt, w, _rope_tables(y_prompt.shape[1]) + (hsum,))
        y_sample = _layer(y_sample, mem_sample, w, _rope_tables(y_sample.shape[1]) + (hsum,))
    return (y_prompt, y_sample)
```

```python
import functools

import jax
import jax.numpy as jnp
import numpy as np
from jax import lax
from jax.experimental import pallas as pl
from jax.experimental.pallas import tpu as pltpu

D_MODEL = 1024
D_CONV = 512
N_HEADS = 16
N_KV_HEADS = 4
GROUP = N_HEADS // N_KV_HEADS
HEAD_DIM = 64
AXIS_DIM = HEAD_DIM // 2
ROPE_THETA = 10000.0
GRID_W = 64
N_MEM = 256
N_MEM_HEADS = 4
MEM_HEAD_DIM = 128
N_BRANCH = 3
D_FF = 2816
EPS = 1e-6

LANES = 128
MXU_N = 256
GROUP_W = GROUP * HEAD_DIM
D_MEM = N_MEM_HEADS * MEM_HEAD_DIM
W_IN_Q = 3 * D_CONV
W_IN_K = W_IN_Q + N_HEADS * HEAD_DIM
W_IN_QM = W_IN_K + 2 * N_KV_HEADS * HEAD_DIM
W_IN_G = W_IN_QM + D_MEM

FFN_TM = 1024
MERGE_TM = 1024
FFN_CK = MXU_N
ROW_CHUNK = 1024
ATTN_TQ = 512
V_ROWS = MXU_N
LOG2_E = 1.4426950408889634

F32 = jnp.float32
BF16 = jnp.bfloat16

_VMEM_LIMIT = 56 * 1024 * 1024


def _resident(shape):
    zeros = (0,) * len(shape)
    return pl.BlockSpec(shape, lambda *_: zeros, pipeline_mode=pl.Buffered(1))


def _rms(x, g):
    ms = jnp.mean(x * x, axis=-1, keepdims=True)
    return x * lax.rsqrt(ms + EPS) * g


def _ffn_kernel(x_ref, pre_ref, w1_ref, w3_ref, w2_ref, post_ref, mixg_ref, o_ref, *rest, emit_u):
    if emit_u:
        u_ref, h_scr = rest
    else:
        (h_scr,) = rest
    x = x_ref[...]
    xn = _rms(x, pre_ref[...]).astype(BF16)
    for c in range(D_FF // FFN_CK):
        cols = slice(c * FFN_CK, (c + 1) * FFN_CK)
        a = jnp.dot(xn, w1_ref[:, cols], preferred_element_type=F32)
        b = jnp.dot(xn, w3_ref[:, cols], preferred_element_type=F32)
        h = a * (1.0 / (1.0 + jnp.exp(-a))) * b
        h_scr[:, cols] = h.astype(BF16)
    y = jnp.dot(h_scr[...], w2_ref[...], preferred_element_type=F32)
    xo = x + _rms(y, post_ref[...])
    o_ref[...] = xo
    if emit_u:
        u_ref[...] = _rms(xo, mixg_ref[...]).astype(BF16)


def _ffn(x2d, pre, w1, w3, w2, post, mixg, emit_u):
    n = x2d.shape[0]
    tm = FFN_TM
    row = lambda i: (i, 0)
    out_shape = [jax.ShapeDtypeStruct((n, D_MODEL), F32)]
    out_specs = [pl.BlockSpec((tm, D_MODEL), row)]
    if emit_u:
        out_shape.append(jax.ShapeDtypeStruct((n, D_MODEL), BF16))
        out_specs.append(pl.BlockSpec((tm, D_MODEL), row))
    res = pl.pallas_call(
        functools.partial(_ffn_kernel, emit_u=emit_u),
        grid=(n // tm,),
        in_specs=[
            pl.BlockSpec((tm, D_MODEL), row),
            _resident((1, D_MODEL)),
            _resident(w1.shape),
            _resident(w3.shape),
            _resident(w2.shape),
            _resident((1, D_MODEL)),
            _resident((1, D_MODEL)),
        ],
        out_specs=out_specs,
        out_shape=out_shape,
        scratch_shapes=[pltpu.VMEM((tm, D_FF), BF16)],
        compiler_params=pltpu.CompilerParams(
            dimension_semantics=("arbitrary",), vmem_limit_bytes=_VMEM_LIMIT),
        name="ffn_u" if emit_u else "ffn",
    )(x2d, pre, w1, w3, w2, post, mixg)
    return res


def _pair_swap(x):
    lane = lax.broadcasted_iota(jnp.int32, x.shape, 1)
    nxt = pltpu.roll(x, LANES - 1, 1)
    prv = pltpu.roll(x, 1, 1)
    return jnp.where((lane & 1) == 0, nxt, prv)


def _norm_rope(r, ssum_ref, g_ref, cos, sin, scale):
    r2 = (r * r).astype(BF16)
    rows = r.shape[0]
    n_blk = r.shape[1] // GROUP_W
    stacked = jnp.concatenate([r2[:, j * GROUP_W:(j + 1) * GROUP_W] for j in range(n_blk)], axis=0)
    sums = jnp.dot(stacked, ssum_ref[...], preferred_element_type=F32)
    ss = jnp.concatenate([sums[j * rows:(j + 1) * rows] for j in range(n_blk)], axis=1)
    rinv = lax.rsqrt(ss * (1.0 / HEAD_DIM) + EPS) * scale
    xg = r * g_ref[...]
    outs = []
    for j in range(r.shape[1] // LANES):
        xs = xg[:, j * LANES:(j + 1) * LANES]
        outs.append(xs * cos + _pair_swap(xs) * sin)
    return jnp.concatenate(outs, axis=1) * rinv


def _group_tile(x, g):
    pair = x[:, (g // 2) * LANES:(g // 2 + 1) * LANES]
    rot = pltpu.roll(pair, HEAD_DIM, 1)
    lane = lax.broadcasted_iota(jnp.int32, pair.shape, 1)
    lo = lane < HEAD_DIM
    t = jnp.where(lo, pair, rot) if g % 2 == 0 else jnp.where(lo, rot, pair)
    return jnp.concatenate([t, t], axis=1)


def _inproj_kernel(u_ref, mem_ref, win_ref, wmkv_ref,
                   convw_ref, convb_ref, qg_ref, kg_ref, memg_ref, cos_ref, sin_ref, hsum_ref,
                   cbz_ref, q_ref, kt_ref, vt_ref, qm_ref, km_ref, vm_ref,
                   z_scr, cb_scr):
    wc_ref = win_ref.at[:, W_IN_Q - 3 * D_CONV:W_IN_Q]
    wq_ref = win_ref.at[:, W_IN_Q:W_IN_K]
    wkv_ref = win_ref.at[:, W_IN_K:W_IN_QM]
    wqm_ref = win_ref.at[:, W_IN_QM:W_IN_G]
    t_len = cbz_ref.shape[0]
    n_chunks = t_len // ROW_CHUNK
    j = pl.program_id(1)

    @pl.when(j == 0)
    def _():
        mn = _rms(mem_ref[...], memg_ref[...]).astype(BF16)
        kv = jnp.dot(mn, wmkv_ref[...], preferred_element_type=F32)
        km_ref[...] = kv[:, :D_MEM].astype(BF16)
        vm_ref[...] = kv[:, D_MEM:].astype(BF16)
        zero_rows = jnp.zeros((8, D_CONV), F32)
        z_scr[0:8, :] = zero_rows
        z_scr[t_len + 8:t_len + 16, :] = zero_rows

    r0 = pl.multiple_of(j * ROW_CHUNK, ROW_CHUNK)
    uc = u_ref[...]

    rc = jnp.dot(uc, wc_ref[...], preferred_element_type=F32)
    z_scr[pl.ds(r0 + 8, ROW_CHUNK), :] = rc[:, 2 * D_CONV:] * rc[:, :D_CONV]
    cb_scr[pl.ds(r0, ROW_CHUNK), :] = rc[:, D_CONV:2 * D_CONV].astype(BF16)

    cos = cos_ref[pl.ds(r0, ROW_CHUNK), :]
    sin = sin_ref[pl.ds(r0, ROW_CHUNK), :]
    rq = jnp.dot(uc, wq_ref[...], preferred_element_type=F32)
    q_ref[...] = _norm_rope(rq, hsum_ref, qg_ref, cos, sin, HEAD_DIM ** -0.5 * LOG2_E).astype(BF16)

    rkv = jnp.dot(uc, wkv_ref[...], preferred_element_type=F32)
    kk = _norm_rope(rkv[:, :GROUP_W], hsum_ref, kg_ref, cos, sin, 1.0)
    vvt = rkv[:, GROUP_W:].T
    tail = (lax.broadcasted_iota(jnp.int32, (V_ROWS - HEAD_DIM, ROW_CHUNK), 0) == 0).astype(BF16)
    for g in range(N_KV_HEADS):
        kt_ref[g] = _group_tile(kk, g).astype(BF16)
        vt_ref[g, 0:HEAD_DIM, :] = vvt[g * HEAD_DIM:(g + 1) * HEAD_DIM, :].astype(BF16)
        vt_ref[g, HEAD_DIM:V_ROWS, :] = tail

    rqm = jnp.dot(uc, wqm_ref[...], preferred_element_type=F32)
    qm_ref[...] = (rqm * (MEM_HEAD_DIM ** -0.5)).astype(BF16)

    @pl.when(j == n_chunks - 1)
    def _():
        cw = convw_ref[...]
        for c in range(n_chunks):
            c0 = c * ROW_CHUNK
            zc = (z_scr[c0 + 7:c0 + 7 + ROW_CHUNK, :] * cw[0:1, :]
                  + z_scr[c0 + 8:c0 + 8 + ROW_CHUNK, :] * cw[1:2, :]
                  + z_scr[c0 + 9:c0 + 9 + ROW_CHUNK, :] * cw[2:3, :]
                  + convb_ref[...])
            cbz_ref[c0:c0 + ROW_CHUNK, :] = (cb_scr[c0:c0 + ROW_CHUNK, :].astype(F32) * zc).astype(BF16)


def _inproj(u, mem, w, tables):
    b, t, _ = u.shape
    cos, sin, hsum = tables
    rows = lambda width: pl.BlockSpec((None, ROW_CHUNK, width), lambda i, j: (i, j, 0))
    tiled = pl.BlockSpec((None, N_KV_HEADS, ROW_CHUNK, GROUP_W), lambda i, j: (i, 0, j, 0))
    vspec = pl.BlockSpec((None, N_KV_HEADS, V_ROWS, ROW_CHUNK), lambda i, j: (i, 0, 0, j))
    memspec = pl.BlockSpec((None, N_MEM, D_MEM), lambda i, j: (i, 0, 0))
    return pl.pallas_call(
        _inproj_kernel,
        grid=(b, t // ROW_CHUNK),
        in_specs=[
            rows(D_MODEL),
            pl.BlockSpec((None, N_MEM, D_MODEL), lambda i, j: (i, 0, 0)),
            _resident(w["w_in"].shape), _resident(w["w_mkv"].shape),
            _resident((3, D_CONV)), _resident((1, D_CONV)),
            _resident((1, N_HEADS * HEAD_DIM)), _resident((1, GROUP_W)), _resident((1, D_MODEL)),
            _resident(cos.shape), _resident(sin.shape), _resident(hsum.shape),
        ],
        out_specs=[pl.BlockSpec((None, t, D_CONV), lambda i, j: (i, 0, 0)), rows(N_HEADS * HEAD_DIM),
                   tiled, vspec, rows(D_MEM), memspec, memspec],
        out_shape=[
            jax.ShapeDtypeStruct((b, t, D_CONV), BF16),
            jax.ShapeDtypeStruct((b, t, N_HEADS * HEAD_DIM), BF16),
            jax.ShapeDtypeStruct((b, N_KV_HEADS, t, GROUP_W), BF16),
            jax.ShapeDtypeStruct((b, N_KV_HEADS, V_ROWS, t), BF16),
            jax.ShapeDtypeStruct((b, t, D_MEM), BF16),
            jax.ShapeDtypeStruct((b, N_MEM, D_MEM), BF16),
            jax.ShapeDtypeStruct((b, N_MEM, D_MEM), BF16),
        ],
        scratch_shapes=[pltpu.VMEM((t + 16, D_CONV), F32), pltpu.VMEM((t, D_CONV), BF16)],
        compiler_params=pltpu.CompilerParams(
            dimension_semantics=("arbitrary", "arbitrary"), vmem_limit_bytes=_VMEM_LIMIT),
        name="inproj",
    )(u, mem, w["w_in"], w["w_mkv"],
      w["conv_w"], w["conv_b"], w["q_gain"], w["k_gain"], w["mem_norm"], cos, sin, hsum)


def _attn_step(q, kt_ref, vt_ref, s_new, m_new, s_cur, m_cur):
    tq = q.shape[0]
    t = s_new.shape[0]
    cols = GROUP * tq
    qf = q.astype(F32)
    head = lax.broadcasted_iota(jnp.int32, qf.shape, 1) // HEAD_DIM
    qs = jnp.concatenate([jnp.where(head == h, qf, 0.0) for h in range(GROUP)], axis=0).astype(BF16)
    st = lax.dot_general(kt_ref[...], qs, (((1,), (1,)), ((), ())), preferred_element_type=F32)
    s_new[:, 0:cols] = st
    m_new[...] = jnp.max(st, axis=0, keepdims=True)

    m = m_cur[...]
    accs = []
    for c in range(cols // MXU_N):
        csl = slice(c * MXU_N, (c + 1) * MXU_N)
        acc = None
        for k in range(t // MXU_N):
            ksl = slice(k * MXU_N, (k + 1) * MXU_N)
            p = jnp.exp2(s_cur[ksl, csl] - m[:, csl]).astype(BF16)
            d = jnp.dot(vt_ref[:, ksl], p, preferred_element_type=F32)
            acc = d if acc is None else acc + d
        accs.append(acc)
    ot = jnp.concatenate(accs, axis=1)
    on = ot[0:HEAD_DIM, :] * (1.0 / ot[HEAD_DIM:HEAD_DIM + 1, :])
    out_t = jnp.concatenate([on[:, h * tq:(h + 1) * tq] for h in range(GROUP)], axis=0)
    return out_t.T.astype(BF16)


def _attn_kernel(q_ref, kt_ref, vt_ref, o_ref, s0, s1, m0, m1):
    n = pl.program_id(0)

    @pl.when(n == 0)
    def _():
        s1[...] = jnp.zeros_like(s1)
        m1[...] = jnp.zeros_like(m1)

    @pl.when(n % 2 == 0)
    def _():
        o_ref[...] = _attn_step(q_ref[...], kt_ref, vt_ref, s0, m0, s1, m1)

    @pl.when(n % 2 == 1)
    def _():
        o_ref[...] = _attn_step(q_ref[...], kt_ref, vt_ref, s1, m1, s0, m0)


def _attn(q, kt, vt):
    b, t, _ = q.shape
    tq = ATTN_TQ
    tiles_per_seq = t // tq
    n_tiles = b * N_KV_HEADS * tiles_per_seq

    def place(i):
        return i // (N_KV_HEADS * tiles_per_seq), (i // tiles_per_seq) % N_KV_HEADS, i % tiles_per_seq

    def score_tile(n):
        return place(jnp.minimum(n, n_tiles - 1))

    def value_tile(n):
        return place(jnp.maximum(n - 1, 0))

    def q_map(n):
        bi, g, j = score_tile(n)
        return bi, j, g

    def kt_map(n):
        bi, g, _ = score_tile(n)
        return bi, g, 0, 0

    def vt_map(n):
        bi, g, _ = value_tile(n)
        return bi, g, 0, 0

    def o_map(n):
        bi, g, j = value_tile(n)
        return bi, j, g

    cols = GROUP * tq
    return pl.pallas_call(
        _attn_kernel,
        grid=(n_tiles + 1,),
        in_specs=[pl.BlockSpec((None, tq, GROUP_W), q_map),
                  pl.BlockSpec((None, None, t, GROUP_W), kt_map),
                  pl.BlockSpec((None, None, V_ROWS, t), vt_map)],
        out_specs=pl.BlockSpec((None, tq, GROUP_W), o_map),
        out_shape=jax.ShapeDtypeStruct((b, t, N_HEADS * HEAD_DIM), BF16),
        scratch_shapes=[pltpu.VMEM((t, cols + LANES), F32), pltpu.VMEM((t, cols + LANES), F32),
                        pltpu.VMEM((1, cols), F32), pltpu.VMEM((1, cols), F32)],
        compiler_params=pltpu.CompilerParams(
            dimension_semantics=("arbitrary",), vmem_limit_bytes=_VMEM_LIMIT),
        name="attn",
    )(q, kt, vt)


def _merge_kernel(x_ref, cbz_ref, attn_ref, qm_ref, km_ref, vm_ref,
                  pconv_ref, pattn_ref, pmem_ref, wg_ref, bg_ref, wout_ref, mixg_ref, postg_ref, o_ref):
    x = x_ref[...]
    u = _rms(x, mixg_ref[...]).astype(BF16)

    qm = qm_ref[...]
    heads = []
    for h in range(N_MEM_HEADS):
        cols = slice(h * MEM_HEAD_DIM, (h + 1) * MEM_HEAD_DIM)
        s = lax.dot_general(qm[:, cols], km_ref[:, cols], (((1,), (1,)), ((), ())),
                            preferred_element_type=F32)
        p = jnp.exp(s - jnp.max(s, axis=-1, keepdims=True))
        l = jnp.sum(p, axis=-1, keepdims=True)
        oh = jnp.dot(p.astype(BF16), vm_ref[:, cols], preferred_element_type=F32)
        heads.append(oh * (1.0 / l))
    memo = jnp.concatenate(heads, axis=1).astype(BF16)

    branches = (
        (cbz_ref[...], pconv_ref),
        (attn_ref[...], pattn_ref),
        (memo, pmem_ref),
    )
    merged = None
    for br, (act, proj_ref) in enumerate(branches):
        cols = slice(br * D_MODEL, (br + 1) * D_MODEL)
        gate = jnp.dot(u, wg_ref[:, cols], preferred_element_type=F32) + bg_ref[:, cols]
        gate = 1.0 / (1.0 + jnp.exp(-gate))
        term = gate * jnp.dot(act, proj_ref[...], preferred_element_type=F32)
        merged = term if merged is None else merged + term
    y = jnp.dot(merged.astype(BF16), wout_ref[...], preferred_element_type=F32)
    o_ref[...] = x + _rms(y, postg_ref[...])


def _merge(x, cbz, attn, qm, km, vm, w):
    b, t, _ = x.shape
    tm = MERGE_TM
    tok = lambda width: pl.BlockSpec((None, tm, width), lambda i, j: (i, j, 0))
    memspec = pl.BlockSpec((None, N_MEM, D_MEM), lambda i, j: (i, 0, 0))
    return pl.pallas_call(
        _merge_kernel,
        grid=(b, t // tm),
        in_specs=[
            tok(D_MODEL), tok(D_CONV), tok(N_HEADS * HEAD_DIM), tok(D_MEM), memspec, memspec,
            _resident(w["p_conv"].shape), _resident(w["p_attn"].shape), _resident(w["p_mem"].shape),
            _resident(w["w_gate"].shape), _resident((1, N_BRANCH * D_MODEL)), _resident(w["w_out"].shape),
            _resident((1, D_MODEL)), _resident((1, D_MODEL)),
        ],
        out_specs=tok(D_MODEL),
        out_shape=jax.ShapeDtypeStruct((b, t, D_MODEL), F32),
        compiler_params=pltpu.CompilerParams(
            dimension_semantics=("arbitrary", "arbitrary"), vmem_limit_bytes=_VMEM_LIMIT),
        name="merge",
    )(x, cbz, attn, qm, km, vm, w["p_conv"], w["p_attn"], w["p_mem"], w["w_gate"], w["b_gate"],
      w["w_out"], w["mix_pre"], w["mix_post"])


def _rope_tables(t):
    pos = np.arange(t)
    row = (pos // GRID_W).astype(np.float64)
    col = (pos % GRID_W).astype(np.float64)
    inv = 1.0 / (ROPE_THETA ** (np.arange(0, AXIS_DIM, 2, dtype=np.float64) / AXIS_DIM))
    ang = np.concatenate([row[:, None] * inv, col[:, None] * inv], axis=-1)
    cos = np.repeat(np.cos(ang), 2, axis=-1)
    sin = np.stack([-np.sin(ang), np.sin(ang)], axis=-1).reshape(t, HEAD_DIM)
    reps = LANES // HEAD_DIM
    return (jnp.asarray(np.tile(cos, (1, reps)), dtype=F32), jnp.asarray(np.tile(sin, (1, reps)), dtype=F32))


def _head_sum_matrix(width):
    head = np.arange(width) // HEAD_DIM
    return jnp.asarray(head[:, None] == head[None, :], dtype=BF16)


def _layer(x, mem, w, tables):
    b, t, d = x.shape
    x1, u = _ffn(x.reshape(b * t, d), w["ffn1_pre"], w["ffn1_w1"], w["ffn1_w3"], w["ffn1_w2"], w["ffn1_post"],
                 w["mix_pre"], emit_u=True)
    cbz, q, kt, vt, qm, km, vm = _inproj(u.reshape(b, t, d), mem, w, tables)
    attn = _attn(q, kt, vt)
    x2 = _merge(x1.reshape(b, t, d), cbz, attn, qm, km, vm, w)
    (y,) = _ffn(x2.reshape(b * t, d), w["ffn2_pre"], w["ffn2_w1"], w["ffn2_w3"], w["ffn2_w2"], w["ffn2_post"],
                w["mix_pre"], emit_u=False)
    return y.reshape(b, t, d)


def kernel(x_prompt, x_sample, mem_prompt, mem_sample, ffn1_pre, ffn1_w1, ffn1_w3, ffn1_w2, ffn1_post, mix_pre, w_in, conv_w, conv_b, p_conv, q_norm, k_norm, p_attn, mem_norm, w_mem_kv, p_mem, b_gate, w_out, mix_post, ffn2_pre, ffn2_w1, ffn2_w3, ffn2_w2, ffn2_post):
    y_prompt, y_sample = x_prompt, x_sample
    for l in range(ffn1_pre.shape[0]):
        w = {
            "ffn1_pre": ffn1_pre[l][None], "ffn1_post": 0.5 * ffn1_post[l][None],
            "ffn1_w1": ffn1_w1[l].astype(BF16), "ffn1_w3": ffn1_w3[l].astype(BF16),
            "ffn1_w2": ffn1_w2[l].astype(BF16),
            "ffn2_pre": ffn2_pre[l][None], "ffn2_post": 0.5 * ffn2_post[l][None],
            "ffn2_w1": ffn2_w1[l].astype(BF16), "ffn2_w3": ffn2_w3[l].astype(BF16),
            "ffn2_w2": ffn2_w2[l].astype(BF16),
            "mix_pre": mix_pre[l][None], "mix_post": mix_post[l][None],
            "w_in": w_in[l, :, :W_IN_G].astype(BF16), "w_gate": w_in[l, :, W_IN_G:].astype(BF16),
            "b_gate": b_gate[l][None],
            "conv_w": conv_w[l], "conv_b": conv_b[l][None],
            "q_gain": jnp.tile(q_norm[l], N_HEADS)[None], "k_gain": jnp.tile(k_norm[l], N_KV_HEADS)[None],
            "mem_norm": mem_norm[l][None], "w_mkv": w_mem_kv[l].astype(BF16),
            "p_conv": p_conv[l].astype(BF16), "p_attn": p_attn[l].astype(BF16), "p_mem": p_mem[l].astype(BF16),
            "w_out": w_out[l].astype(BF16),
        }
        hsum = _head_sum_matrix(GROUP_W)
        y_prompt = _layer(y_prompt, mem_prompt, w, _rope_tables(y_prompt.shape[1]) + (hsum,))
        y_sample = _layer(y_sample, mem_sample, w, _rope_tables(y_sample.shape[1]) + (hsum,))
    return (y_prompt, y_sample)
```
